```python
import jax, jax.numpy as jnp
from jax import lax
import numpy as np

D_MODEL = 1024
BATCH = 8
SEQ = 4096
DEPTH = 2

N_MIXERS = 2
DN_HEADS = 8
DN_HEAD_DIM = 128
DN_INNER = DN_HEADS * DN_HEAD_DIM
DN_CONV = 4
DN_CHUNK = 64
DN_IN_COLS = 4 * DN_INNER + 2 * DN_HEADS
CV_WIDTH = 31
MEM_LEN = 256
XA_HEADS = 4
XA_HEAD_DIM = D_MODEL // XA_HEADS
D_FF = 4 * D_MODEL
N_DN_LAYERS = (DEPTH + 1) // 2
N_CV_LAYERS = DEPTH // 2
RMS_EPS = 1e-6
LN_EPS = 1e-5

kernel_name = "hybrid_deltanet_conformer_xattn_trunk"


def rms_norm(x, g):
    xf = x.astype(jnp.float32)
    y = xf * lax.rsqrt(jnp.mean(xf * xf, axis=-1, keepdims=True) + RMS_EPS)
    return (y * g.astype(jnp.float32)).astype(x.dtype)


def layer_norm(x, g, b):
    xf = x.astype(jnp.float32)
    mu = jnp.mean(xf, axis=-1, keepdims=True)
    xc = xf - mu
    y = xc * lax.rsqrt(jnp.mean(xc * xc, axis=-1, keepdims=True) + LN_EPS)
    return (y * g.astype(jnp.float32) + b.astype(jnp.float32)).astype(x.dtype)


def l2_normalize(x):
    return x * lax.rsqrt(jnp.sum(x * x, axis=-1, keepdims=True) + 1e-6)


def causal_depthwise_conv(x, w):
    width = w.shape[0]
    return lax.conv_general_dilated(
        x, w[:, None, :].astype(x.dtype), window_strides=(1,), padding=[(width - 1, 0)],
        dimension_numbers=("NWC", "WIO", "NWC"), feature_group_count=x.shape[-1])


def chunk_gated_delta_rule(q, k, v, g, beta):
    B, S, H, dk = q.shape
    dv = v.shape[-1]
    n = S // DN_CHUNK

    def blocks(t):
        return t.reshape(B, n, DN_CHUNK, H, t.shape[-1]).transpose(0, 3, 1, 2, 4)

    q, k, v = blocks(q), blocks(k), blocks(v)
    g = g.reshape(B, n, DN_CHUNK, H).transpose(0, 3, 1, 2)
    beta = beta.reshape(B, n, DN_CHUNK, H).transpose(0, 3, 1, 2)
    g_cum = jnp.cumsum(g, axis=-1)

    idx = jnp.arange(DN_CHUNK)
    causal = idx[:, None] >= idx[None, :]
    strict = idx[:, None] > idx[None, :]
    diff = g_cum[..., :, None] - g_cum[..., None, :]
    decay = jnp.exp(jnp.where(causal, diff, -jnp.inf))

    k_beta = k * beta[..., None]
    lower = jnp.where(strict, jnp.einsum("bhnik,bhnjk->bhnij", k_beta, k) * decay, 0.0)
    a_mat = jnp.eye(DN_CHUNK, dtype=q.dtype) + lower
    rhs = jnp.concatenate([v * beta[..., None], k_beta * jnp.exp(g_cum)[..., None]], axis=-1)
    sol = lax.linalg.triangular_solve(a_mat, rhs, left_side=True, lower=True, unit_diagonal=True)
    u, w = sol[..., :dv], sol[..., dv:]

    attn_intra = jnp.einsum("bhnik,bhnjk->bhnij", q, k) * decay
    q_dec = q * jnp.exp(g_cum)[..., None]
    g_last = g_cum[..., -1]
    k_dec = k * jnp.exp(g_last[..., None] - g_cum)[..., None]

    xs = tuple(jnp.moveaxis(t, 2, 0) for t in (q_dec, k_dec, u, w, attn_intra, g_last))

    def step(state, inp):
        qd, kd, uc, wc, ai, gl = inp
        v_new = uc - jnp.einsum("bhck,bhkv->bhcv", wc, state)
        o = jnp.einsum("bhck,bhkv->bhcv", qd, state) + jnp.einsum("bhcj,bhjv->bhcv", ai, v_new)
        state = state * jnp.exp(gl)[..., None, None] + jnp.einsum("bhck,bhcv->bhkv", kd, v_new)
        return state, o

    s0 = jnp.zeros((B, H, dk, dv), jnp.float32)
    _, o = lax.scan(step, s0, xs)
    return o.transpose(1, 0, 3, 2, 4).reshape(B, S, H, dv)


def gated_deltanet(h, w_in, w_conv, a_log, dt_bias, out_norm, w_out):
    B, S, _ = h.shape
    proj = h @ w_in
    qkv = proj[..., :3 * DN_INNER]
    z = proj[..., 3 * DN_INNER:4 * DN_INNER]
    b_raw = proj[..., 4 * DN_INNER:4 * DN_INNER + DN_HEADS].astype(jnp.float32)
    a_raw = proj[..., 4 * DN_INNER + DN_HEADS:].astype(jnp.float32)
    qkv = jax.nn.silu(causal_depthwise_conv(qkv, w_conv)).astype(jnp.float32)
    q = qkv[..., :DN_INNER].reshape(B, S, DN_HEADS, DN_HEAD_DIM)
    k = qkv[..., DN_INNER:2 * DN_INNER].reshape(B, S, DN_HEADS, DN_HEAD_DIM)
    v = qkv[..., 2 * DN_INNER:].reshape(B, S, DN_HEADS, DN_HEAD_DIM)
    q = l2_normalize(q) * (DN_HEAD_DIM ** -0.5)
    k = l2_normalize(k)
    beta = jax.nn.sigmoid(b_raw)
    g = -jnp.exp(a_log.astype(jnp.float32)) * jax.nn.softplus(a_raw + dt_bias.astype(jnp.float32))
    o = chunk_gated_delta_rule(q, k, v, g, beta)
    o = o * lax.rsqrt(jnp.mean(o * o, axis=-1, keepdims=True) + RMS_EPS) * out_norm.astype(jnp.float32)
    o = o * jax.nn.silu(z.astype(jnp.float32).reshape(B, S, DN_HEADS, DN_HEAD_DIM))
    return o.astype(h.dtype).reshape(B, S, DN_INNER) @ w_out


def conformer_conv(h, w_pw1, b_pw1, w_dw, b_dw, ln_g, ln_b, w_pw2, b_pw2):
    u = h @ w_pw1 + b_pw1
    u = u[..., :D_MODEL] * jax.nn.sigmoid(u[..., D_MODEL:])
    c = causal_depthwise_conv(u, w_dw) + b_dw
    c = jax.nn.silu(layer_norm(c, ln_g, ln_b))
    return c @ w_pw2 + b_pw2


def memory_cross_attention(h, mem_h, w_q, w_kv, w_o):
    B, S, _ = h.shape
    M = mem_h.shape[1]
    q = (h @ w_q).reshape(B, S, XA_HEADS, XA_HEAD_DIM)
    kv = (mem_h @ w_kv).reshape(B, M, 2, XA_HEADS, XA_HEAD_DIM)
    k, v = kv[:, :, 0], kv[:, :, 1]
    s = jnp.einsum("bshd,bmhd->bhsm", q, k).astype(jnp.float32) * (XA_HEAD_DIM ** -0.5)
    p = jax.nn.softmax(s, axis=-1).astype(v.dtype)
    o = jnp.einsum("bhsm,bmhd->bshd", p, v).reshape(B, S, D_MODEL)
    return o @ w_o


def sq_relu_mlp(h, w_up, w_down):
    return jnp.square(jax.nn.relu(h @ w_up)) @ w_down


def setup_inputs(seed: int = 0) -> dict:
    key = jax.random.key(seed)
    ks = jax.random.split(key, 32)
    f32 = jnp.float32

    def nrm(k, shape, scale):
        return jax.random.normal(k, shape, f32) * scale

    def gain(k, shape):
        return 1.0 + 0.02 * jax.random.normal(k, shape, f32)

    dt = jax.random.uniform(ks[6], (N_DN_LAYERS, DN_HEADS), f32, 1e-3, 1e-1)
    return {
        "x": jax.random.normal(ks[0], (BATCH, SEQ, D_MODEL), f32),
        "mem": jax.random.normal(ks[1], (BATCH, MEM_LEN, D_MODEL), f32),
        "dn_norm": gain(ks[2], (N_DN_LAYERS, D_MODEL)),
        "dn_w_in": nrm(ks[3], (N_DN_LAYERS, D_MODEL, DN_IN_COLS), D_MODEL ** -0.5),
        "dn_w_conv": nrm(ks[4], (N_DN_LAYERS, DN_CONV, 3 * DN_INNER), DN_CONV ** -0.5),
        "dn_a_log": jnp.log(jax.random.uniform(ks[5], (N_DN_LAYERS, DN_HEADS), f32, 1.0, 16.0)),
        "dn_dt_bias": dt + jnp.log(-jnp.expm1(-dt)),
        "dn_out_norm": gain(ks[7], (N_DN_LAYERS, DN_HEAD_DIM)),
        "dn_w_out": nrm(ks[8], (N_DN_LAYERS, DN_INNER, D_MODEL), DN_INNER ** -0.5),
        "cv_norm": gain(ks[9], (N_CV_LAYERS, D_MODEL)),
        "cv_w_pw1": nrm(ks[10], (N_CV_LAYERS, D_MODEL, 2 * D_MODEL), D_MODEL ** -0.5),
        "cv_b_pw1": nrm(ks[11], (N_CV_LAYERS, 2 * D_MODEL), 0.02),
        "cv_w_dw": nrm(ks[12], (N_CV_LAYERS, CV_WIDTH, D_MODEL), CV_WIDTH ** -0.5),
        "cv_b_dw": nrm(ks[13], (N_CV_LAYERS, D_MODEL), 0.02),
        "cv_ln_g": gain(ks[14], (N_CV_LAYERS, D_MODEL)),
        "cv_ln_b": nrm(ks[15], (N_CV_LAYERS, D_MODEL), 0.02),
        "cv_w_pw2": nrm(ks[16], (N_CV_LAYERS, D_MODEL, D_MODEL), D_MODEL ** -0.5),
        "cv_b_pw2": nrm(ks[17], (N_CV_LAYERS, D_MODEL), 0.02),
        "xa_norm": gain(ks[18], (DEPTH, D_MODEL)),
        "xa_mem_norm": gain(ks[19], (DEPTH, D_MODEL)),
        "xa_w_q": nrm(ks[20], (DEPTH, D_MODEL, D_MODEL), D_MODEL ** -0.5),
        "xa_w_kv": nrm(ks[21], (DEPTH, D_MODEL, 2 * D_MODEL), D_MODEL ** -0.5),
        "xa_w_o": nrm(ks[22], (DEPTH, D_MODEL, D_MODEL), D_MODEL ** -0.5),
        "mlp_norm": gain(ks[23], (DEPTH, D_MODEL)),
        "mlp_w_up": nrm(ks[24], (DEPTH, D_MODEL, D_FF), D_MODEL ** -0.5),
        "mlp_w_down": nrm(ks[25], (DEPTH, D_FF, D_MODEL), D_FF ** -0.5),
        "final_norm": gain(ks[26], (D_MODEL,)),
    }


def reference(x, mem, dn_norm, dn_w_in, dn_w_conv, dn_a_log, dn_dt_bias, dn_out_norm, dn_w_out,
              cv_norm, cv_w_pw1, cv_b_pw1, cv_w_dw, cv_b_dw, cv_ln_g, cv_ln_b, cv_w_pw2, cv_b_pw2,
              xa_norm, xa_mem_norm, xa_w_q, xa_w_kv, xa_w_o, mlp_norm, mlp_w_up, mlp_w_down,
              final_norm):
    h = x
    for layer in range(DEPTH):
        j = layer // N_MIXERS
        if layer % N_MIXERS == 0:
            h = h + gated_deltanet(rms_norm(h, dn_norm[j]), dn_w_in[j], dn_w_conv[j], dn_a_log[j],
                                   dn_dt_bias[j], dn_out_norm[j], dn_w_out[j])
        else:
            h = h + conformer_conv(rms_norm(h, cv_norm[j]), cv_w_pw1[j], cv_b_pw1[j], cv_w_dw[j],
                                   cv_b_dw[j], cv_ln_g[j], cv_ln_b[j], cv_w_pw2[j], cv_b_pw2[j])
        h = h + memory_cross_attention(rms_norm(h, xa_norm[layer]), rms_norm(mem, xa_mem_norm[layer]),
                                       xa_w_q[layer], xa_w_kv[layer], xa_w_o[layer])
        h = h + sq_relu_mlp(rms_norm(h, mlp_norm[layer]), mlp_w_up[layer], mlp_w_down[layer])
    return rms_norm(h, final_norm)
```

```python
import functools

import jax
import jax.numpy as jnp
from jax import lax
from jax.experimental import pallas as pl
from jax.experimental.pallas import tpu as pltpu

F32 = jnp.float32
BF16 = jnp.bfloat16

D_MODEL = 1024
DN_HEADS = 8
DN_HEAD_DIM = 128
DN_INNER = DN_HEADS * DN_HEAD_DIM
DN_CONV = 4
DN_CHUNK = 64
DN_TILE = 256
DN_PROJ_COLS = 4 * DN_INNER + 128
CV_WIDTH = 31
CV_HALO = 32
XA_HEADS = 4
XA_HEAD_DIM = D_MODEL // XA_HEADS
RMS_EPS = 1e-6
LN_EPS = 1e-5
L2_EPS = 1e-6

VMEM_LIMIT_BYTES = 56 * 1024 * 1024


def _params(*semantics):
    return pltpu.CompilerParams(dimension_semantics=semantics, vmem_limit_bytes=VMEM_LIMIT_BYTES)


def _sigmoid(x):
    return 1.0 / (1.0 + jnp.exp(-x))


def _rms_norm(x, g):
    ms = jnp.mean(x * x, axis=-1, keepdims=True)
    return x * lax.rsqrt(ms + RMS_EPS) * g


def _dot(a, b):
    return jnp.dot(a, b, preferred_element_type=F32)


def _dot_nt(a, b):
    return lax.dot_general(a, b, (((1,), (1,)), ((), ())), preferred_element_type=F32)


def _dot_tn(a, b):
    return lax.dot_general(a, b, (((0,), (0,)), ((), ())), preferred_element_type=F32)


def _rms_matmul_kernel(x_ref, g_ref, w_ref, o_ref, xn_ref):
    @pl.when(pl.program_id(1) == 0)
    def _():
        xn_ref[...] = _rms_norm(x_ref[...], g_ref[...]).astype(BF16)

    o_ref[...] = _dot(xn_ref[...], w_ref[...]).astype(o_ref.dtype)


def rms_matmul(x, g, w, *, tm, tn, out_dtype, name):
    m, k = x.shape
    n = w.shape[1]
    return pl.pallas_call(
        _rms_matmul_kernel,
        grid=(m // tm, n // tn),
        in_specs=[
            pl.BlockSpec((tm, k), lambda i, j: (i, 0)),
            pl.BlockSpec((1, k), lambda i, j: (0, 0)),
            pl.BlockSpec((k, tn), lambda i, j: (0, j)),
        ],
        out_specs=pl.BlockSpec((tm, tn), lambda i, j: (i, j)),
        out_shape=jax.ShapeDtypeStruct((m, n), out_dtype),
        scratch_shapes=[pltpu.VMEM((tm, k), BF16)],
        compiler_params=_params("parallel", "arbitrary"),
        name=name,
    )(x, g.reshape(1, k), w)


def _rms_glu_kernel(x_ref, g_ref, wa_ref, wb_ref, ba_ref, bb_ref, o_ref, xn_ref):
    @pl.when(pl.program_id(1) == 0)
    def _():
        xn_ref[...] = _rms_norm(x_ref[...], g_ref[...]).astype(BF16)

    xn = xn_ref[...]
    a = _dot(xn, wa_ref[...]) + ba_ref[...]
    b = _dot(xn, wb_ref[...]) + bb_ref[...]
    o_ref[...] = a * _sigmoid(b)


def rms_glu(x, g, wa, wb, ba, bb, *, tm, tn):
    m, k = x.shape
    n = wa.shape[1]
    return pl.pallas_call(
        _rms_glu_kernel,
        grid=(m // tm, n // tn),
        in_specs=[
            pl.BlockSpec((tm, k), lambda i, j: (i, 0)),
            pl.BlockSpec((1, k), lambda i, j: (0, 0)),
            pl.BlockSpec((k, tn), lambda i, j: (0, j)),
            pl.BlockSpec((k, tn), lambda i, j: (0, j)),
            pl.BlockSpec((1, tn), lambda i, j: (0, j)),
            pl.BlockSpec((1, tn), lambda i, j: (0, j)),
        ],
        out_specs=pl.BlockSpec((tm, tn), lambda i, j: (i, j)),
        out_shape=jax.ShapeDtypeStruct((m, n), F32),
        scratch_shapes=[pltpu.VMEM((tm, k), BF16)],
        compiler_params=_params("parallel", "arbitrary"),
        name="cv_pw1_glu",
    )(x, g.reshape(1, k), wa, wb, ba.reshape(1, n), bb.reshape(1, n))


def _matmul_res_kernel(a_ref, w_ref, r_ref, o_ref):
    o_ref[...] = r_ref[...] + _dot(a_ref[...], w_ref[...])


def matmul_res(a, w, res, *, tm, name):
    m, k = a.shape
    n = w.shape[1]
    return pl.pallas_call(
        _matmul_res_kernel,
        grid=(m // tm,),
        in_specs=[
            pl.BlockSpec((tm, k), lambda i: (i, 0)),
            pl.BlockSpec((k, n), lambda i: (0, 0)),
            pl.BlockSpec((tm, n), lambda i: (i, 0)),
        ],
        out_specs=pl.BlockSpec((tm, n), lambda i: (i, 0)),
        out_shape=jax.ShapeDtypeStruct((m, n), F32),
        compiler_params=_params("parallel"),
        name=name,
    )(a, w, res)


def _mlp_kernel(x_ref, g_ref, wu_ref, wd_ref, gf_ref, o_ref, xn_ref, *, final_norm):
    j = pl.program_id(1)

    @pl.when(j == 0)
    def _():
        x = x_ref[...]
        xn_ref[...] = _rms_norm(x, g_ref[...]).astype(BF16)
        o_ref[...] = x

    hid = _dot(xn_ref[...], wu_ref[...])
    hid = jnp.square(jnp.maximum(hid, 0.0)).astype(BF16)
    o_ref[...] += _dot(hid, wd_ref[...])

    if final_norm:
        @pl.when(j == pl.num_programs(1) - 1)
        def _():
            o_ref[...] = _rms_norm(o_ref[...], gf_ref[...])


def mlp(x, g, wu, wd, gf, *, tm, tf, final_norm, name):
    m, k = x.shape
    f = wu.shape[1]
    return pl.pallas_call(
        functools.partial(_mlp_kernel, final_norm=final_norm),
        grid=(m // tm, f // tf),
        in_specs=[
            pl.BlockSpec((tm, k), lambda i, j: (i, 0)),
            pl.BlockSpec((1, k), lambda i, j: (0, 0)),
            pl.BlockSpec((k, tf), lambda i, j: (0, j)),
            pl.BlockSpec((tf, k), lambda i, j: (j, 0)),
            pl.BlockSpec((1, k), lambda i, j: (0, 0)),
        ],
        out_specs=pl.BlockSpec((tm, k), lambda i, j: (i, 0)),
        out_shape=jax.ShapeDtypeStruct((m, k), F32),
        scratch_shapes=[pltpu.VMEM((tm, k), BF16)],
        compiler_params=_params("parallel", "arbitrary"),
        name=name,
    )(x, g.reshape(1, k), wu, wd, gf.reshape(1, k))


def _xattn_kernel(x_ref, g_ref, wq_ref, k_ref, v_ref, wo_ref, o_ref):
    x = x_ref[0]
    xn = _rms_norm(x, g_ref[...]).astype(BF16)
    q = (_dot(xn, wq_ref[...]) * (XA_HEAD_DIM ** -0.5)).astype(BF16)
    outs = []
    for hd in range(XA_HEADS):
        cols = slice(hd * XA_HEAD_DIM, (hd + 1) * XA_HEAD_DIM)
        s = _dot_nt(q[:, cols], k_ref[0, :, cols])
        p = jnp.exp(s - jnp.max(s, axis=-1, keepdims=True))
        denom = jnp.sum(p, axis=-1, keepdims=True)
        outs.append((_dot(p.astype(BF16), v_ref[0, :, cols]) / denom).astype(BF16))
    o = jnp.concatenate(outs, axis=-1)
    o_ref[0] = x + _dot(o, wo_ref[...])


def xattn(x, g, wq, kv, wo, *, tm, name):
    b, s, d = x.shape
    mlen = kv.shape[1]
    return pl.pallas_call(
        _xattn_kernel,
        grid=(b, s // tm),
        in_specs=[
            pl.BlockSpec((1, tm, d), lambda bi, i: (bi, i, 0)),
            pl.BlockSpec((1, d), lambda bi, i: (0, 0)),
            pl.BlockSpec((d, d), lambda bi, i: (0, 0)),
            pl.BlockSpec((1, mlen, d), lambda bi, i: (bi, 0, 0)),
            pl.BlockSpec((1, mlen, d), lambda bi, i: (bi, 0, 1)),
            pl.BlockSpec((d, d), lambda bi, i: (0, 0)),
        ],
        out_specs=pl.BlockSpec((1, tm, d), lambda bi, i: (bi, i, 0)),
        out_shape=jax.ShapeDtypeStruct((b, s, d), F32),
        compiler_params=_params("parallel", "parallel"),
        name=name,
    )(x, g.reshape(1, d), wq, kv, kv, wo)


CV_ROWS = 64


def _cv_kernel(u_ref, halo_ref, res_ref, wdw_ref, bdw_ref, lg_ref, lb_ref, w2_ref, b2_ref,
               o_ref, win_ref, c_ref):
    ts = u_ref.shape[1]
    first = pl.program_id(1) == 0
    win_ref[0:CV_HALO, :] = jnp.where(first, 0.0, halo_ref[0])
    win_ref[CV_HALO:, :] = u_ref[0]
    base = CV_HALO - (CV_WIDTH - 1)
    for cs in range(D_MODEL // 128):
        cols = slice(cs * 128, (cs + 1) * 128)
        for r0 in range(0, ts, CV_ROWS):
            acc = jnp.broadcast_to(bdw_ref[0:1, cols], (CV_ROWS, 128))
            for j in range(CV_WIDTH):
                acc = acc + wdw_ref[j:j + 1, cols] * win_ref[r0 + base + j:r0 + base + j + CV_ROWS, cols]
            c_ref[r0:r0 + CV_ROWS, cols] = acc

    c = c_ref[...]
    mu = jnp.mean(c, axis=-1, keepdims=True)
    xc = c - mu
    y = xc * lax.rsqrt(jnp.mean(xc * xc, axis=-1, keepdims=True) + LN_EPS) * lg_ref[...] + lb_ref[...]
    y = (y * _sigmoid(y)).astype(BF16)
    o_ref[0] = res_ref[0] + _dot(y, w2_ref[...]) + b2_ref[...]


def conformer_tail(u, res, wdw, bdw, lg, lb, w2, b2, *, ts):
    b, s, d = u.shape
    halo_blocks = ts // CV_HALO
    row = lambda v: v.reshape(1, d)
    return pl.pallas_call(
        _cv_kernel,
        grid=(b, s // ts),
        in_specs=[
            pl.BlockSpec((1, ts, d), lambda bi, i: (bi, i, 0)),
            pl.BlockSpec((1, CV_HALO, d), lambda bi, i: (bi, jnp.maximum(i * halo_blocks - 1, 0), 0)),
            pl.BlockSpec((1, ts, d), lambda bi, i: (bi, i, 0)),
            pl.BlockSpec((CV_WIDTH, d), lambda bi, i: (0, 0)),
            pl.BlockSpec((1, d), lambda bi, i: (0, 0)),
            pl.BlockSpec((1, d), lambda bi, i: (0, 0)),
            pl.BlockSpec((1, d), lambda bi, i: (0, 0)),
            pl.BlockSpec((d, d), lambda bi, i: (0, 0)),
            pl.BlockSpec((1, d), lambda bi, i: (0, 0)),
        ],
        out_specs=pl.BlockSpec((1, ts, d), lambda bi, i: (bi, i, 0)),
        out_shape=jax.ShapeDtypeStruct((b, s, d), F32),
        scratch_shapes=[pltpu.VMEM((ts + CV_HALO, d), F32), pltpu.VMEM((ts, d), F32)],
        compiler_params=_params("parallel", "arbitrary"),
        name="cv_conv_ln_pw2",
    )(u, u, res, wdw, row(bdw), row(lg), row(lb), w2, row(b2))


def _split_dot(a, b):
    ah = a.astype(BF16)
    bh = b.astype(BF16)
    al = (a - ah.astype(F32)).astype(BF16)
    bl = (b - bh.astype(F32)).astype(BF16)
    return _dot(ah, bh) + _dot(ah, bl) + _dot(al, bh)


def _delta_kernel(qkv_ref, z_ref, ba_ref, wc_ref, gp_ref, on_ref, o_ref, win_ref, state_ref):
    nt = DN_TILE
    nchunks = nt // DN_CHUNK

    @pl.when(pl.program_id(1) == 0)
    def _():
        win_ref[0:8, :] = jnp.zeros((8, 3 * DN_INNER), F32)
        state_ref[...] = jnp.zeros(state_ref.shape, F32)

    win_ref[8:8 + nt, :] = qkv_ref[...]

    ba = ba_ref[...]
    beta_all = _sigmoid(ba)
    ap = ba + gp_ref[1:2, :]
    softplus = jnp.maximum(ap, 0.0) + jnp.log1p(jnp.exp(-jnp.abs(ap)))
    g_all = -jnp.exp(gp_ref[0:1, :]) * softplus
    pos = lax.broadcasted_iota(jnp.int32, (nt, 128), 0) % DN_CHUNK
    gc_all = g_all
    rs_all = g_all
    shift = 1
    while shift < DN_CHUNK:
        gc_all = gc_all + jnp.where(pos >= shift, pltpu.roll(gc_all, shift, axis=0), 0.0)
        rs_all = rs_all + jnp.where(pos < DN_CHUNK - shift, pltpu.roll(rs_all, nt - shift, axis=0), 0.0)
        shift *= 2
    rem_all = rs_all - g_all
    gc_rows = jnp.transpose(gc_all)

    ri = lax.broadcasted_iota(jnp.int32, (nt, nt), 0)
    ci = lax.broadcasted_iota(jnp.int32, (nt, nt), 1)
    same_chunk = (ri // DN_CHUNK) == (ci // DN_CHUNK)
    causal = same_chunk & (ri >= ci)
    strict = same_chunk & (ri > ci)
    eye = (ri == ci).astype(F32)

    def conv_silu(col0):
        cols = slice(col0, col0 + DN_HEAD_DIM)
        acc = wc_ref[0:1, cols] * win_ref[5:5 + nt, cols]
        for j in range(1, DN_CONV):
            acc = acc + wc_ref[j:j + 1, cols] * win_ref[5 + j:5 + j + nt, cols]
        return acc * _sigmoid(acc)

    for h in range(DN_HEADS):
        beta = beta_all[:, h:h + 1]
        gcc = gc_all[:, DN_HEADS + h:DN_HEADS + h + 1]
        rem = rem_all[:, DN_HEADS + h:DN_HEADS + h + 1]
        gcr = gc_rows[DN_HEADS + h:DN_HEADS + h + 1, :]

        q = conv_silu(h * DN_HEAD_DIM)
        k = conv_silu(DN_INNER + h * DN_HEAD_DIM)
        v = conv_silu(2 * DN_INNER + h * DN_HEAD_DIM)
        q = q * lax.rsqrt(jnp.sum(q * q, axis=-1, keepdims=True) + L2_EPS) * (DN_HEAD_DIM ** -0.5)
        k = k * lax.rsqrt(jnp.sum(k * k, axis=-1, keepdims=True) + L2_EPS)

        decay = jnp.exp(jnp.where(causal, gcc - gcr, -jnp.inf))
        egc = jnp.exp(gcc)
        kb = k * beta
        kbf = k.astype(BF16)
        lower = jnp.where(strict, _dot_nt(kb.astype(BF16), kbf) * decay, 0.0)

        inv = eye - lower
        power = lower
        order = 2
        while order < DN_CHUNK:
            power = _split_dot(power, power)
            inv = inv + _split_dot(inv, power)
            order *= 2

        rhs = jnp.concatenate([v * beta, kb * egc], axis=1)
        sol = _split_dot(inv, rhs)
        u = sol[:, :DN_HEAD_DIM]
        w = sol[:, DN_HEAD_DIM:]

        attn = _dot_nt(q.astype(BF16), kbf) * decay
        qd = (q * egc).astype(BF16)
        kd = (k * jnp.exp(rem)).astype(BF16)
        wbf = w.astype(BF16)

        state = state_ref[h]
        v_new = []
        o_inter = []
        for c in range(nchunks):
            rows = slice(c * DN_CHUNK, (c + 1) * DN_CHUNK)
            both = _dot(jnp.concatenate([wbf[rows], qd[rows]], axis=0), state.astype(BF16))
            vn = u[rows] - both[:DN_CHUNK]
            o_inter.append(both[DN_CHUNK:])
            v_new.append(vn)
            last = (c + 1) * DN_CHUNK - 1
            state = state * jnp.exp(gcc[last:last + 1, :]) + _dot_tn(kd[rows], vn.astype(BF16))
        state_ref[h] = state

        v_new = jnp.concatenate(v_new, axis=0)
        o = jnp.concatenate(o_inter, axis=0) + _dot(attn.astype(BF16), v_new.astype(BF16))
        o = o * lax.rsqrt(jnp.mean(o * o, axis=-1, keepdims=True) + RMS_EPS) * on_ref[...]
        zc = z_ref[:, h * DN_HEAD_DIM:(h + 1) * DN_HEAD_DIM]
        o_ref[:, h * DN_HEAD_DIM:(h + 1) * DN_HEAD_DIM] = (o * (zc * _sigmoid(zc))).astype(BF16)

    win_ref[0:8, :] = qkv_ref[nt - 8:nt, :]


def delta_core(proj, wc, gate_params, out_norm, *, batch, seq):
    tiles = seq // DN_TILE
    qkv_cols = 3 * DN_INNER
    return pl.pallas_call(
        _delta_kernel,
        grid=(batch, tiles),
        in_specs=[
            pl.BlockSpec((DN_TILE, qkv_cols), lambda b, i: (b * tiles + i, 0)),
            pl.BlockSpec((DN_TILE, DN_INNER), lambda b, i: (b * tiles + i, qkv_cols // DN_INNER)),
            pl.BlockSpec((DN_TILE, 128), lambda b, i: (b * tiles + i, 4 * DN_INNER // 128)),
            pl.BlockSpec((DN_CONV, qkv_cols), lambda b, i: (0, 0)),
            pl.BlockSpec((8, 128), lambda b, i: (0, 0)),
            pl.BlockSpec((1, DN_HEAD_DIM), lambda b, i: (0, 0)),
        ],
        out_specs=pl.BlockSpec((DN_TILE, DN_INNER), lambda b, i: (b * tiles + i, 0)),
        out_shape=jax.ShapeDtypeStruct((batch * seq, DN_INNER), BF16),
        scratch_shapes=[
            pltpu.VMEM((DN_TILE + 8, qkv_cols), F32),
            pltpu.VMEM((DN_HEADS, DN_HEAD_DIM, DN_HEAD_DIM), F32),
        ],
        compiler_params=_params("parallel", "arbitrary"),
        name="dn_delta_core",
    )(proj, proj, proj, wc, gate_params, out_norm.reshape(1, DN_HEAD_DIM))


ROW_TILE = 1024
XA_TILE = 512
CV_TILE = 512


def _deltanet_layer(h2d, norm, w_in, w_conv, a_log, dt_bias, out_norm, w_out, *, batch, seq):
    w_in_p = jnp.pad(w_in, ((0, 0), (0, DN_PROJ_COLS - w_in.shape[1]))).astype(BF16)
    proj = rms_matmul(h2d, norm, w_in_p, tm=ROW_TILE, tn=DN_PROJ_COLS // 3, out_dtype=F32,
                      name="dn_in_proj")
    gate_params = jnp.zeros((8, 128), F32)
    gate_params = gate_params.at[0, DN_HEADS:2 * DN_HEADS].set(a_log)
    gate_params = gate_params.at[1, DN_HEADS:2 * DN_HEADS].set(dt_bias)
    o = delta_core(proj, w_conv, gate_params, out_norm, batch=batch, seq=seq)
    return matmul_res(o, w_out.astype(BF16), h2d, tm=ROW_TILE, name="dn_out_proj")


def _conformer_layer(h2d, norm, w_pw1, b_pw1, w_dw, b_dw, ln_g, ln_b, w_pw2, b_pw2, *, batch, seq):
    w1 = w_pw1.astype(BF16)
    u = rms_glu(h2d, norm, w1[:, :D_MODEL], w1[:, D_MODEL:], b_pw1[:D_MODEL], b_pw1[D_MODEL:],
                tm=ROW_TILE, tn=D_MODEL)
    out = conformer_tail(u.reshape(batch, seq, D_MODEL), h2d.reshape(batch, seq, D_MODEL),
                         w_dw, b_dw, ln_g, ln_b, w_pw2.astype(BF16), b_pw2, ts=CV_TILE)
    return out.reshape(batch * seq, D_MODEL)


def _xattn_layer(h2d, mem2d, norm, mem_norm, w_q, w_kv, w_o, *, batch, seq, name):
    mlen = mem2d.shape[0] // batch
    kv = rms_matmul(mem2d, mem_norm, w_kv.astype(BF16), tm=ROW_TILE, tn=ROW_TILE, out_dtype=BF16,
                    name=name + "_kv")
    out = xattn(h2d.reshape(batch, seq, D_MODEL), norm, w_q.astype(BF16),
                kv.reshape(batch, mlen, 2 * D_MODEL), w_o.astype(BF16), tm=XA_TILE, name=name)
    return out.reshape(batch * seq, D_MODEL)


def kernel(x, mem, dn_norm, dn_w_in, dn_w_conv, dn_a_log, dn_dt_bias, dn_out_norm, dn_w_out, cv_norm, cv_w_pw1, cv_b_pw1, cv_w_dw, cv_b_dw, cv_ln_g, cv_ln_b, cv_w_pw2, cv_b_pw2, xa_norm, xa_mem_norm, xa_w_q, xa_w_kv, xa_w_o, mlp_norm, mlp_w_up, mlp_w_down, final_norm):
    batch, seq, d = x.shape
    depth = xa_norm.shape[0]
    h = x.reshape(batch * seq, d)
    mem2d = mem.reshape(batch * mem.shape[1], d)
    for layer in range(depth):
        j = layer // 2
        if layer % 2 == 0:
            h = _deltanet_layer(h, dn_norm[j], dn_w_in[j], dn_w_conv[j], dn_a_log[j], dn_dt_bias[j],
                                dn_out_norm[j], dn_w_out[j], batch=batch, seq=seq)
        else:
            h = _conformer_layer(h, cv_norm[j], cv_w_pw1[j], cv_b_pw1[j], cv_w_dw[j], cv_b_dw[j],
                                 cv_ln_g[j], cv_ln_b[j], cv_w_pw2[j], cv_b_pw2[j], batch=batch, seq=seq)
        h = _xattn_layer(h, mem2d, xa_norm[layer], xa_mem_norm[layer], xa_w_q[layer], xa_w_kv[layer],
                         xa_w_o[layer], batch=batch, seq=seq, name=f"xattn{layer}")
        last = layer == depth - 1
        h = mlp(h, mlp_norm[layer], mlp_w_up[layer].astype(BF16), mlp_w_down[layer].astype(BF16),
                final_norm, tm=ROW_TILE, tf=1024, final_norm=last, name=f"mlp{layer}")
    return h.reshape(batch, seq, d)
```

```python
import functools

import jax
import jax.numpy as jnp
from jax import lax
from jax.experimental import pallas as pl
from jax.experimental.pallas import tpu as pltpu

F32 = jnp.float32
BF16 = jnp.bfloat16

D_MODEL = 1024
DN_HEADS = 8
DN_HEAD_DIM = 128
DN_INNER = DN_HEADS * DN_HEAD_DIM
DN_CONV = 4
DN_CHUNK = 64
DN_TILE = 256
DN_PROJ_COLS = 4 * DN_INNER + 128
CV_WIDTH = 31
CV_HALO = 32
XA_HEADS = 4
XA_HEAD_DIM = D_MODEL // XA_HEADS
RMS_EPS = 1e-6
LN_EPS = 1e-5
L2_EPS = 1e-6

VMEM_LIMIT_BYTES = 56 * 1024 * 1024


def _params(*semantics):
    return pltpu.CompilerParams(dimension_semantics=semantics, vmem_limit_bytes=VMEM_LIMIT_BYTES)


def _sigmoid(x):
    return 1.0 / (1.0 + jnp.exp(-x))


def _rms_norm(x, g):
    ms = jnp.mean(x * x, axis=-1, keepdims=True)
    return x * lax.rsqrt(ms + RMS_EPS) * g


def _dot(a, b):
    return jnp.dot(a, b, preferred_element_type=F32)


def _dot_nt(a, b):
    return lax.dot_general(a, b, (((1,), (1,)), ((), ())), preferred_element_type=F32)


def _dot_tn(a, b):
    return lax.dot_general(a, b, (((0,), (0,)), ((), ())), preferred_element_type=F32)


def _rms_matmul_kernel(x_ref, g_ref, w_ref, o_ref, xn_ref):
    @pl.when(pl.program_id(1) == 0)
    def _():
        xn_ref[...] = _rms_norm(x_ref[...], g_ref[...]).astype(BF16)

    o_ref[...] = _dot(xn_ref[...], w_ref[...]).astype(o_ref.dtype)


def rms_matmul(x, g, w, *, tm, tn, out_dtype, name):
    m, k = x.shape
    n = w.shape[1]
    return pl.pallas_call(
        _rms_matmul_kernel,
        grid=(m // tm, n // tn),
        in_specs=[
            pl.BlockSpec((tm, k), lambda i, j: (i, 0)),
            pl.BlockSpec((1, k), lambda i, j: (0, 0)),
            pl.BlockSpec((k, tn), lambda i, j: (0, j)),
        ],
        out_specs=pl.BlockSpec((tm, tn), lambda i, j: (i, j)),
        out_shape=jax.ShapeDtypeStruct((m, n), out_dtype),
        scratch_shapes=[pltpu.VMEM((tm, k), BF16)],
        compiler_params=_params("parallel", "arbitrary"),
        name=name,
    )(x, g.reshape(1, k), w)


def _rms_glu_kernel(x_ref, g_ref, wa_ref, wb_ref, ba_ref, bb_ref, o_ref, xn_ref):
    @pl.when(pl.program_id(1) == 0)
    def _():
        xn_ref[...] = _rms_norm(x_ref[...], g_ref[...]).astype(BF16)

    xn = xn_ref[...]
    a = _dot(xn, wa_ref[...]) + ba_ref[...]
    b = _dot(xn, wb_ref[...]) + bb_ref[...]
    o_ref[...] = a * _sigmoid(b)


def rms_glu(x, g, wa, wb, ba, bb, *, tm, tn):
    m, k = x.shape
    n = wa.shape[1]
    return pl.pallas_call(
        _rms_glu_kernel,
        grid=(m // tm, n // tn),
        in_specs=[
            pl.BlockSpec((tm, k), lambda i, j: (i, 0)),
            pl.BlockSpec((1, k), lambda i, j: (0, 0)),
            pl.BlockSpec((k, tn), lambda i, j: (0, j)),
            pl.BlockSpec((k, tn), lambda i, j: (0, j)),
            pl.BlockSpec((1, tn), lambda i, j: (0, j)),
            pl.BlockSpec((1, tn), lambda i, j: (0, j)),
        ],
        out_specs=pl.BlockSpec((tm, tn), lambda i, j: (i, j)),
        out_shape=jax.ShapeDtypeStruct((m, n), F32),
        scratch_shapes=[pltpu.VMEM((tm, k), BF16)],
        compiler_params=_params("parallel", "arbitrary"),
        name="cv_pw1_glu",
    )(x, g.reshape(1, k), wa, wb, ba.reshape(1, n), bb.reshape(1, n))


def _matmul_res_kernel(a_ref, w_ref, r_ref, o_ref):
    o_ref[...] = r_ref[...] + _dot(a_ref[...], w_ref[...])


def matmul_res(a, w, res, *, tm, name):
    m, k = a.shape
    n = w.shape[1]
    return pl.pallas_call(
        _matmul_res_kernel,
        grid=(m // tm,),
        in_specs=[
            pl.BlockSpec((tm, k), lambda i: (i, 0)),
            pl.BlockSpec((k, n), lambda i: (0, 0)),
            pl.BlockSpec((tm, n), lambda i: (i, 0)),
        ],
        out_specs=pl.BlockSpec((tm, n), lambda i: (i, 0)),
        out_shape=jax.ShapeDtypeStruct((m, n), F32),
        compiler_params=_params("parallel"),
        name=name,
    )(a, w, res)


def _mlp_kernel(x_ref, g_ref, wu_ref, wd_ref, gf_ref, o_ref, xn_ref, *, final_norm):
    j = pl.program_id(1)

    @pl.when(j == 0)
    def _():
        x = x_ref[...]
        xn_ref[...] = _rms_norm(x, g_ref[...]).astype(BF16)
        o_ref[...] = x

    hid = _dot(xn_ref[...], wu_ref[...])
    hid = jnp.square(jnp.maximum(hid, 0.0)).astype(BF16)
    o_ref[...] += _dot(hid, wd_ref[...])

    if final_norm:
        @pl.when(j == pl.num_programs(1) - 1)
        def _():
            o_ref[...] = _rms_norm(o_ref[...], gf_ref[...])


def mlp(x, g, wu, wd, gf, *, tm, tf, final_norm, name):
    m, k = x.shape
    f = wu.shape[1]
    return pl.pallas_call(
        functools.partial(_mlp_kernel, final_norm=final_norm),
        grid=(m // tm, f // tf),
        in_specs=[
            pl.BlockSpec((tm, k), lambda i, j: (i, 0)),
            pl.BlockSpec((1, k), lambda i, j: (0, 0)),
            pl.BlockSpec((k, tf), lambda i, j: (0, j)),
            pl.BlockSpec((tf, k), lambda i, j: (j, 0)),
            pl.BlockSpec((1, k), lambda i, j: (0, 0)),
        ],
        out_specs=pl.BlockSpec((tm, k), lambda i, j: (i, 0)),
        out_shape=jax.ShapeDtypeStruct((m, k), F32),
        scratch_shapes=[pltpu.VMEM((tm, k), BF16)],
        compiler_params=_params("parallel", "arbitrary"),
        name=name,
    )(x, g.reshape(1, k), wu, wd, gf.reshape(1, k))


def _xattn_kernel(x_ref, g_ref, wq_ref, k_ref, v_ref, wo_ref, o_ref):
    x = x_ref[0]
    xn = _rms_norm(x, g_ref[...]).astype(BF16)
    q = (_dot(xn, wq_ref[...]) * (XA_HEAD_DIM ** -0.5)).astype(BF16)
    outs = []
    for hd in range(XA_HEADS):
        cols = slice(hd * XA_HEAD_DIM, (hd + 1) * XA_HEAD_DIM)
        s = _dot_nt(q[:, cols], k_ref[0, :, cols])
        p = jnp.exp(s - jnp.max(s, axis=-1, keepdims=True))
        denom = jnp.sum(p, axis=-1, keepdims=True)
        outs.append((_dot(p.astype(BF16), v_ref[0, :, cols]) / denom).astype(BF16))
    o = jnp.concatenate(outs, axis=-1)
    o_ref[0] = x + _dot(o, wo_ref[...])


def xattn(x, g, wq, kv, wo, *, tm, name):
    b, s, d = x.shape
    mlen = kv.shape[1]
    return pl.pallas_call(
        _xattn_kernel,
        grid=(b, s // tm),
        in_specs=[
            pl.BlockSpec((1, tm, d), lambda bi, i: (bi, i, 0)),
            pl.BlockSpec((1, d), lambda bi, i: (0, 0)),
            pl.BlockSpec((d, d), lambda bi, i: (0, 0)),
            pl.BlockSpec((1, mlen, d), lambda bi, i: (bi, 0, 0)),
            pl.BlockSpec((1, mlen, d), lambda bi, i: (bi, 0, 1)),
            pl.BlockSpec((d, d), lambda bi, i: (0, 0)),
        ],
        out_specs=pl.BlockSpec((1, tm, d), lambda bi, i: (bi, i, 0)),
        out_shape=jax.ShapeDtypeStruct((b, s, d), F32),
        compiler_params=_params("parallel", "parallel"),
        name=name,
    )(x, g.reshape(1, d), wq, kv, kv, wo)


CV_ROWS = 64


def _cv_kernel(u_ref, halo_ref, res_ref, wdw_ref, bdw_ref, lg_ref, lb_ref, w2_ref, b2_ref,
               o_ref, win_ref, c_ref):
    ts = u_ref.shape[1]
    first = pl.program_id(1) == 0
    win_ref[0:CV_HALO, :] = jnp.where(first, 0.0, halo_ref[0])
    win_ref[CV_HALO:, :] = u_ref[0]
    base = CV_HALO - (CV_WIDTH - 1)
    for cs in range(D_MODEL // 128):
        cols = slice(cs * 128, (cs + 1) * 128)
        for r0 in range(0, ts, CV_ROWS):
            acc = jnp.broadcast_to(bdw_ref[0:1, cols], (CV_ROWS, 128))
            for j in range(CV_WIDTH):
                acc = acc + wdw_ref[j:j + 1, cols] * win_ref[r0 + base + j:r0 + base + j + CV_ROWS, cols]
            c_ref[r0:r0 + CV_ROWS, cols] = acc

    c = c_ref[...]
    mu = jnp.mean(c, axis=-1, keepdims=True)
    xc = c - mu
    y = xc * lax.rsqrt(jnp.mean(xc * xc, axis=-1, keepdims=True) + LN_EPS) * lg_ref[...] + lb_ref[...]
    y = (y * _sigmoid(y)).astype(BF16)
    o_ref[0] = res_ref[0] + _dot(y, w2_ref[...]) + b2_ref[...]


def conformer_tail(u, res, wdw, bdw, lg, lb, w2, b2, *, ts):
    b, s, d = u.shape
    halo_blocks = ts // CV_HALO
    row = lambda v: v.reshape(1, d)
    return pl.pallas_call(
        _cv_kernel,
        grid=(b, s // ts),
        in_specs=[
            pl.BlockSpec((1, ts, d), lambda bi, i: (bi, i, 0)),
            pl.BlockSpec((1, CV_HALO, d), lambda bi, i: (bi, jnp.maximum(i * halo_blocks - 1, 0), 0)),
            pl.BlockSpec((1, ts, d), lambda bi, i: (bi, i, 0)),
            pl.BlockSpec((CV_WIDTH, d), lambda bi, i: (0, 0)),
            pl.BlockSpec((1, d), lambda bi, i: (0, 0)),
            pl.BlockSpec((1, d), lambda bi, i: (0, 0)),
            pl.BlockSpec((1, d), lambda bi, i: (0, 0)),
            pl.BlockSpec((d, d), lambda bi, i: (0, 0)),
            pl.BlockSpec((1, d), lambda bi, i: (0, 0)),
        ],
        out_specs=pl.BlockSpec((1, ts, d), lambda bi, i: (bi, i, 0)),
        out_shape=jax.ShapeDtypeStruct((b, s, d), F32),
        scratch_shapes=[pltpu.VMEM((ts + CV_HALO, d), F32), pltpu.VMEM((ts, d), F32)],
        compiler_params=_params("parallel", "arbitrary"),
        name="cv_conv_ln_pw2",
    )(u, u, res, wdw, row(bdw), row(lg), row(lb), w2, row(b2))


def _delta_kernel(qkv_ref, z_ref, ba_ref, wc_ref, gp_ref, on_ref, o_ref, win_ref, state_ref):
    nt = DN_TILE
    ch = DN_CHUNK
    nchunks = nt // ch
    heads = range(DN_HEADS)

    @pl.when(pl.program_id(1) == 0)
    def _():
        win_ref[0:8, :] = jnp.zeros((8, 3 * DN_INNER), F32)
        state_ref[...] = jnp.zeros(state_ref.shape, F32)

    win_ref[8:8 + nt, :] = qkv_ref[...]

    ba = ba_ref[...]
    beta_all = _sigmoid(ba)
    ap = ba + gp_ref[1:2, :]
    softplus = jnp.maximum(ap, 0.0) + jnp.log1p(jnp.exp(-jnp.abs(ap)))
    g_all = -jnp.exp(gp_ref[0:1, :]) * softplus
    pos = lax.broadcasted_iota(jnp.int32, (nt, 128), 0) % DN_CHUNK
    gc_all = g_all
    rs_all = g_all
    shift = 1
    while shift < DN_CHUNK:
        gc_all = gc_all + jnp.where(pos >= shift, pltpu.roll(gc_all, shift, axis=0), 0.0)
        rs_all = rs_all + jnp.where(pos < DN_CHUNK - shift, pltpu.roll(rs_all, nt - shift, axis=0), 0.0)
        shift *= 2
    rem_all = rs_all - g_all
    gc_rows = jnp.transpose(gc_all)

    ri = lax.broadcasted_iota(jnp.int32, (nt, nt), 0)
    ci = lax.broadcasted_iota(jnp.int32, (nt, nt), 1)
    same_chunk = (ri // DN_CHUNK) == (ci // DN_CHUNK)
    causal = same_chunk & (ri >= ci)
    strict = same_chunk & (ri > ci)
    eye_side = (lax.broadcasted_iota(jnp.int32, (ch, nt), 0)
                == lax.broadcasted_iota(jnp.int32, (ch, nt), 1) % ch).astype(F32)

    def to_blockdiag(x_side):
        return jnp.where(same_chunk, jnp.concatenate([x_side] * nchunks, axis=0), 0.0).astype(BF16)

    def fold(x_bd):
        acc = x_bd[0:ch]
        for c in range(1, nchunks):
            acc = acc + x_bd[c * ch:(c + 1) * ch]
        return acc

    def conv_silu(col0):
        cols = slice(col0, col0 + DN_HEAD_DIM)
        acc = wc_ref[0:1, cols] * win_ref[5:5 + nt, cols]
        for j in range(1, DN_CONV):
            acc = acc + wc_ref[j:j + 1, cols] * win_ref[5 + j:5 + j + nt, cols]
        return acc * _sigmoid(acc)

    gcc, lower, attn, rhs, qd, kd = [], [], [], [], [], []
    for h in heads:
        beta = beta_all[:, h:h + 1]
        gcc.append(gc_all[:, DN_HEADS + h:DN_HEADS + h + 1])
        rem = rem_all[:, DN_HEADS + h:DN_HEADS + h + 1]
        gcr = gc_rows[DN_HEADS + h:DN_HEADS + h + 1, :]

        q = conv_silu(h * DN_HEAD_DIM)
        k = conv_silu(DN_INNER + h * DN_HEAD_DIM)
        v = conv_silu(2 * DN_INNER + h * DN_HEAD_DIM)
        q = q * lax.rsqrt(jnp.sum(q * q, axis=-1, keepdims=True) + L2_EPS) * (DN_HEAD_DIM ** -0.5)
        k = k * lax.rsqrt(jnp.sum(k * k, axis=-1, keepdims=True) + L2_EPS)

        decay = jnp.exp(jnp.where(causal, gcc[h] - gcr, -jnp.inf))
        egc = jnp.exp(gcc[h])
        kb = k * beta
        both = _dot_nt(jnp.concatenate([kb, q], axis=0).astype(BF16), k.astype(BF16))
        lower.append(fold(jnp.where(strict, both[:nt] * decay, 0.0)))
        attn.append((both[nt:] * decay).astype(BF16))
        rhs.append(jnp.concatenate([v * beta, kb * egc], axis=1))
        qd.append((q * egc).astype(BF16))
        kd.append((k * jnp.exp(rem)).astype(BF16))

    inv = [eye_side - lower[h] for h in heads]
    power = [_dot(lower[h].astype(BF16), to_blockdiag(lower[h])) for h in heads]
    order = 2
    while order < ch:
        last = order * 2 >= ch
        for h in heads:
            wts = to_blockdiag(power[h])
            if last:
                inv[h] = inv[h] + _dot(inv[h].astype(BF16), wts)
            else:
                res = _dot(jnp.concatenate([inv[h], power[h]], axis=0).astype(BF16), wts)
                inv[h] = inv[h] + res[:ch]
                power[h] = res[ch:]
        order *= 2

    u, wbf = [], []
    for h in heads:
        sol = rhs[h] + _dot(to_blockdiag(inv[h] - eye_side), rhs[h].astype(BF16))
        u.append(sol[:, :DN_HEAD_DIM])
        wbf.append(sol[:, DN_HEAD_DIM:].astype(BF16))

    state = [state_ref[h] for h in heads]
    v_new = [[] for _ in heads]
    o_inter = [[] for _ in heads]
    for c in range(nchunks):
        rows = slice(c * ch, (c + 1) * ch)
        last_row = (c + 1) * ch - 1
        for h in heads:
            both = _dot(jnp.concatenate([wbf[h][rows], qd[h][rows]], axis=0), state[h].astype(BF16))
            vn = u[h][rows] - both[:ch]
            o_inter[h].append(both[ch:])
            v_new[h].append(vn)
            state[h] = (state[h] * jnp.exp(gcc[h][last_row:last_row + 1, :])
                        + _dot_tn(kd[h][rows], vn.astype(BF16)))

    for h in heads:
        state_ref[h] = state[h]
        vn = jnp.concatenate(v_new[h], axis=0).astype(BF16)
        o = jnp.concatenate(o_inter[h], axis=0) + _dot(attn[h], vn)
        o = o * lax.rsqrt(jnp.mean(o * o, axis=-1, keepdims=True) + RMS_EPS) * on_ref[...]
        zc = z_ref[:, h * DN_HEAD_DIM:(h + 1) * DN_HEAD_DIM]
        o_ref[:, h * DN_HEAD_DIM:(h + 1) * DN_HEAD_DIM] = (o * (zc * _sigmoid(zc))).astype(BF16)

    win_ref[0:8, :] = qkv_ref[nt - 8:nt, :]


def delta_core(proj, wc, gate_params, out_norm, *, batch, seq):
    tiles = seq // DN_TILE
    qkv_cols = 3 * DN_INNER
    return pl.pallas_call(
        _delta_kernel,
        grid=(batch, tiles),
        in_specs=[
            pl.BlockSpec((DN_TILE, qkv_cols), lambda b, i: (b * tiles + i, 0)),
            pl.BlockSpec((DN_TILE, DN_INNER), lambda b, i: (b * tiles + i, qkv_cols // DN_INNER)),
            pl.BlockSpec((DN_TILE, 128), lambda b, i: (b * tiles + i, 4 * DN_INNER // 128)),
            pl.BlockSpec((DN_CONV, qkv_cols), lambda b, i: (0, 0)),
            pl.BlockSpec((8, 128), lambda b, i: (0, 0)),
            pl.BlockSpec((1, DN_HEAD_DIM), lambda b, i: (0, 0)),
        ],
        out_specs=pl.BlockSpec((DN_TILE, DN_INNER), lambda b, i: (b * tiles + i, 0)),
        out_shape=jax.ShapeDtypeStruct((batch * seq, DN_INNER), BF16),
        scratch_shapes=[
            pltpu.VMEM((DN_TILE + 8, qkv_cols), F32),
            pltpu.VMEM((DN_HEADS, DN_HEAD_DIM, DN_HEAD_DIM), F32),
        ],
        compiler_params=_params("parallel", "arbitrary"),
        name="dn_delta_core",
    )(proj, proj, proj, wc, gate_params, out_norm.reshape(1, DN_HEAD_DIM))


ROW_TILE = 1024
XA_TILE = 512
CV_TILE = 512


def _deltanet_layer(h2d, norm, w_in, w_conv, a_log, dt_bias, out_norm, w_out, *, batch, seq):
    w_in_p = jnp.pad(w_in, ((0, 0), (0, DN_PROJ_COLS - w_in.shape[1]))).astype(BF16)
    proj = rms_matmul(h2d, norm, w_in_p, tm=ROW_TILE, tn=DN_PROJ_COLS // 3, out_dtype=F32,
                      name="dn_in_proj")
    gate_params = jnp.zeros((8, 128), F32)
    gate_params = gate_params.at[0, DN_HEADS:2 * DN_HEADS].set(a_log)
    gate_params = gate_params.at[1, DN_HEADS:2 * DN_HEADS].set(dt_bias)
    o = delta_core(proj, w_conv, gate_params, out_norm, batch=batch, seq=seq)
    return matmul_res(o, w_out.astype(BF16), h2d, tm=ROW_TILE, name="dn_out_proj")


def _conformer_layer(h2d, norm, w_pw1, b_pw1, w_dw, b_dw, ln_g, ln_b, w_pw2, b_pw2, *, batch, seq):
    w1 = w_pw1.astype(BF16)
    u = rms_glu(h2d, norm, w1[:, :D_MODEL], w1[:, D_MODEL:], b_pw1[:D_MODEL], b_pw1[D_MODEL:],
                tm=ROW_TILE, tn=D_MODEL)
    out = conformer_tail(u.reshape(batch, seq, D_MODEL), h2d.reshape(batch, seq, D_MODEL),
                         w_dw, b_dw, ln_g, ln_b, w_pw2.astype(BF16), b_pw2, ts=CV_TILE)
    return out.reshape(batch * seq, D_MODEL)


def _xattn_layer(h2d, mem2d, norm, mem_norm, w_q, w_kv, w_o, *, batch, seq, name):
    mlen = mem2d.shape[0] // batch
    kv = rms_matmul(mem2d, mem_norm, w_kv.astype(BF16), tm=ROW_TILE, tn=ROW_TILE, out_dtype=BF16,
                    name=name + "_kv")
    out = xattn(h2d.reshape(batch, seq, D_MODEL), norm, w_q.astype(BF16),
                kv.reshape(batch, mlen, 2 * D_MODEL), w_o.astype(BF16), tm=XA_TILE, name=name)
    return out.reshape(batch * seq, D_MODEL)


def kernel(x, mem, dn_norm, dn_w_in, dn_w_conv, dn_a_log, dn_dt_bias, dn_out_norm, dn_w_out, cv_norm, cv_w_pw1, cv_b_pw1, cv_w_dw, cv_b_dw, cv_ln_g, cv_ln_b, cv_w_pw2, cv_b_pw2, xa_norm, xa_mem_norm, xa_w_q, xa_w_kv, xa_w_o, mlp_norm, mlp_w_up, mlp_w_down, final_norm):
    batch, seq, d = x.shape
    depth = xa_norm.shape[0]
    h = x.reshape(batch * seq, d)
    mem2d = mem.reshape(batch * mem.shape[1], d)
    for layer in range(depth):
        j = layer // 2
        if layer % 2 == 0:
            h = _deltanet_layer(h, dn_norm[j], dn_w_in[j], dn_w_conv[j], dn_a_log[j], dn_dt_bias[j],
                                dn_out_norm[j], dn_w_out[j], batch=batch, seq=seq)
        else:
            h = _conformer_layer(h, cv_norm[j], cv_w_pw1[j], cv_b_pw1[j], cv_w_dw[j], cv_b_dw[j],
                                 cv_ln_g[j], cv_ln_b[j], cv_w_pw2[j], cv_b_pw2[j], batch=batch, seq=seq)
        h = _xattn_layer(h, mem2d, xa_norm[layer], xa_mem_norm[layer], xa_w_q[layer], xa_w_kv[layer],
                         xa_w_o[layer], batch=batch, seq=seq, name=f"xattn{layer}")
        last = layer == depth - 1
        h = mlp(h, mlp_norm[layer], mlp_w_up[layer].astype(BF16), mlp_w_down[layer].astype(BF16),
                final_norm, tm=ROW_TILE, tf=1024, final_norm=last, name=f"mlp{layer}")
    return h.reshape(batch, seq, d)
```

```python
import functools

import jax
import jax.numpy as jnp
from jax import lax
from jax.experimental import pallas as pl
from jax.experimental.pallas import tpu as pltpu

F32 = jnp.float32
BF16 = jnp.bfloat16

D_MODEL = 1024
DN_HEADS = 8
DN_HEAD_DIM = 128
DN_INNER = DN_HEADS * DN_HEAD_DIM
DN_CONV = 4
DN_CHUNK = 64
DN_TILE = 256
DN_PROJ_COLS = 4 * DN_INNER + 128
CV_WIDTH = 31
CV_HALO = 32
XA_HEADS = 4
XA_HEAD_DIM = D_MODEL // XA_HEADS
RMS_EPS = 1e-6
LN_EPS = 1e-5
L2_EPS = 1e-6

VMEM_LIMIT_BYTES = 56 * 1024 * 1024


def _params(*semantics):
    return pltpu.CompilerParams(dimension_semantics=semantics, vmem_limit_bytes=VMEM_LIMIT_BYTES)


def _sigmoid(x):
    return 0.5 * jnp.tanh(0.5 * x) + 0.5


def _rms_norm(x, g):
    ms = jnp.mean(x * x, axis=-1, keepdims=True)
    return x * lax.rsqrt(ms + RMS_EPS) * g


def _dot(a, b):
    return jnp.dot(a, b, preferred_element_type=F32)


def _dot_nt(a, b):
    return lax.dot_general(a, b, (((1,), (1,)), ((), ())), preferred_element_type=F32)


def _dot_tn(a, b):
    return lax.dot_general(a, b, (((0,), (0,)), ((), ())), preferred_element_type=F32)


def _rms_matmul_kernel(x_ref, g_ref, w_ref, o_ref, xn_ref):
    @pl.when(pl.program_id(1) == 0)
    def _():
        xn_ref[...] = _rms_norm(x_ref[...], g_ref[...]).astype(BF16)

    o_ref[...] = _dot(xn_ref[...], w_ref[...]).astype(o_ref.dtype)


def rms_matmul(x, g, w, *, tm, tn, out_dtype, name):
    m, k = x.shape
    n = w.shape[1]
    return pl.pallas_call(
        _rms_matmul_kernel,
        grid=(m // tm, n // tn),
        in_specs=[
            pl.BlockSpec((tm, k), lambda i, j: (i, 0)),
            pl.BlockSpec((1, k), lambda i, j: (0, 0)),
            pl.BlockSpec((k, tn), lambda i, j: (0, j)),
        ],
        out_specs=pl.BlockSpec((tm, tn), lambda i, j: (i, j)),
        out_shape=jax.ShapeDtypeStruct((m, n), out_dtype),
        scratch_shapes=[pltpu.VMEM((tm, k), BF16)],
        compiler_params=_params("parallel", "arbitrary"),
        name=name,
    )(x, g.reshape(1, k), w)


def _mlp_kernel(x_ref, g_ref, wu_ref, wd_ref, gf_ref, o_ref, xn_ref, *, final_norm):
    j = pl.program_id(1)

    @pl.when(j == 0)
    def _():
        x = x_ref[...]
        xn_ref[...] = _rms_norm(x, g_ref[...]).astype(BF16)
        o_ref[...] = x

    hid = _dot(xn_ref[...], wu_ref[...])
    hid = jnp.square(jnp.maximum(hid, 0.0)).astype(BF16)
    o_ref[...] += _dot(hid, wd_ref[...])

    if final_norm:
        @pl.when(j == pl.num_programs(1) - 1)
        def _():
            o_ref[...] = _rms_norm(o_ref[...], gf_ref[...])


def mlp(x, g, wu, wd, gf, *, tm, tf, final_norm, name):
    m, k = x.shape
    f = wu.shape[1]
    return pl.pallas_call(
        functools.partial(_mlp_kernel, final_norm=final_norm),
        grid=(m // tm, f // tf),
        in_specs=[
            pl.BlockSpec((tm, k), lambda i, j: (i, 0)),
            pl.BlockSpec((1, k), lambda i, j: (0, 0)),
            pl.BlockSpec((k, tf), lambda i, j: (0, j)),
            pl.BlockSpec((tf, k), lambda i, j: (j, 0)),
            pl.BlockSpec((1, k), lambda i, j: (0, 0)),
        ],
        out_specs=pl.BlockSpec((tm, k), lambda i, j: (i, 0)),
        out_shape=jax.ShapeDtypeStruct((m, k), F32),
        scratch_shapes=[pltpu.VMEM((tm, k), BF16)],
        compiler_params=_params("parallel", "arbitrary"),
        name=name,
    )(x, g.reshape(1, k), wu, wd, gf.reshape(1, k))


def _xattn_kernel(x_ref, g_ref, wq_ref, k_ref, v_ref, wo_ref, o_ref):
    x = x_ref[0]
    xn = _rms_norm(x, g_ref[...]).astype(BF16)
    q = (_dot(xn, wq_ref[...]) * (XA_HEAD_DIM ** -0.5)).astype(BF16)
    outs = []
    for hd in range(XA_HEADS):
        cols = slice(hd * XA_HEAD_DIM, (hd + 1) * XA_HEAD_DIM)
        s = _dot_nt(q[:, cols], k_ref[0, :, cols])
        p = jnp.exp(s - jnp.max(s, axis=-1, keepdims=True))
        denom = jnp.sum(p, axis=-1, keepdims=True)
        outs.append((_dot(p.astype(BF16), v_ref[0, :, cols]) / denom).astype(BF16))
    o = jnp.concatenate(outs, axis=-1)
    o_ref[0] = x + _dot(o, wo_ref[...])


def xattn(x, g, wq, kv, wo, *, tm, name):
    b, s, d = x.shape
    mlen = kv.shape[1]
    return pl.pallas_call(
        _xattn_kernel,
        grid=(b, s // tm),
        in_specs=[
            pl.BlockSpec((1, tm, d), lambda bi, i: (bi, i, 0)),
            pl.BlockSpec((1, d), lambda bi, i: (0, 0)),
            pl.BlockSpec((d, d), lambda bi, i: (0, 0)),
            pl.BlockSpec((1, mlen, d), lambda bi, i: (bi, 0, 0)),
            pl.BlockSpec((1, mlen, d), lambda bi, i: (bi, 0, 1)),
            pl.BlockSpec((d, d), lambda bi, i: (0, 0)),
        ],
        out_specs=pl.BlockSpec((1, tm, d), lambda bi, i: (bi, i, 0)),
        out_shape=jax.ShapeDtypeStruct((b, s, d), F32),
        compiler_params=_params("parallel", "parallel"),
        name=name,
    )(x, g.reshape(1, d), wq, kv, kv, wo)


CV_ROWS = 128


def _cv_kernel(x_ref, g_ref, w1a_ref, w1b_ref, b1a_ref, b1b_ref, wdw_ref, bdw_ref, lg_ref, lb_ref,
               w2_ref, b2_ref, o_ref, win_ref, c_ref):
    ts = x_ref.shape[1]

    @pl.when(pl.program_id(1) == 0)
    def _():
        win_ref[0:CV_HALO, :] = jnp.zeros((CV_HALO, D_MODEL), F32)

    x = x_ref[0]
    xn = _rms_norm(x, g_ref[...]).astype(BF16)
    gate = _dot(xn, w1b_ref[...]) + b1b_ref[...]
    win_ref[CV_HALO:, :] = (_dot(xn, w1a_ref[...]) + b1a_ref[...]) * _sigmoid(gate)
    base = CV_HALO - (CV_WIDTH - 1)
    for cs in range(D_MODEL // 128):
        cols = slice(cs * 128, (cs + 1) * 128)
        for r0 in range(0, ts, CV_ROWS):
            partial = {}
            for j in range(CV_WIDTH):
                start = r0 + base + j
                term = wdw_ref[j:j + 1, cols] * win_ref[start:start + CV_ROWS, cols]
                phase = start % 8
                partial[phase] = partial[phase] + term if phase in partial else term
            acc = partial[0] + bdw_ref[0:1, cols]
            for phase in range(1, 8):
                acc = acc + partial[phase]
            c_ref[r0:r0 + CV_ROWS, cols] = acc

    win_ref[0:CV_HALO, :] = win_ref[ts:ts + CV_HALO, :]

    c = c_ref[...]
    mu = jnp.mean(c, axis=-1, keepdims=True)
    xc = c - mu
    y = xc * lax.rsqrt(jnp.mean(xc * xc, axis=-1, keepdims=True) + LN_EPS) * lg_ref[...] + lb_ref[...]
    y = (y * _sigmoid(y)).astype(BF16)
    o_ref[0] = x_ref[0] + _dot(y, w2_ref[...]) + b2_ref[...]


def conformer(x, g, w1a, w1b, b1a, b1b, wdw, bdw, lg, lb, w2, b2, *, ts):
    b, s, d = x.shape
    row = lambda v: v.reshape(1, d)
    vec = pl.BlockSpec((1, d), lambda bi, i: (0, 0))
    mat = pl.BlockSpec((d, d), lambda bi, i: (0, 0))
    return pl.pallas_call(
        _cv_kernel,
        grid=(b, s // ts),
        in_specs=[
            pl.BlockSpec((1, ts, d), lambda bi, i: (bi, i, 0)),
            vec, mat, mat, vec, vec,
            pl.BlockSpec((CV_WIDTH, d), lambda bi, i: (0, 0)),
            vec, vec, vec, mat, vec,
        ],
        out_specs=pl.BlockSpec((1, ts, d), lambda bi, i: (bi, i, 0)),
        out_shape=jax.ShapeDtypeStruct((b, s, d), F32),
        scratch_shapes=[pltpu.VMEM((ts + CV_HALO, d), F32), pltpu.VMEM((ts, d), F32)],
        compiler_params=_params("parallel", "arbitrary"),
        name="conformer",
    )(x, row(g), w1a, w1b, row(b1a), row(b1b), wdw, row(bdw), row(lg), row(lb), w2, row(b2))


def _delta_kernel(x_ref, g_ref, wi_ref, wc_ref, gp_ref, on_ref, wo_ref, bd_ref, o_ref,
                  win_ref, z_ref, state_ref, gated_ref):
    nt = DN_TILE
    ch = DN_CHUNK
    nchunks = nt // ch
    heads = range(DN_HEADS)

    @pl.when(pl.program_id(1) == 0)
    def _():
        win_ref[0:8, :] = jnp.zeros((8, 3 * DN_INNER), F32)
        state_ref[...] = jnp.zeros(state_ref.shape, F32)

    xn = _rms_norm(x_ref[...], g_ref[...]).astype(BF16)
    for part in range(3):
        cols = slice(part * DN_INNER, (part + 1) * DN_INNER)
        win_ref[8:8 + nt, cols] = _dot(xn, wi_ref[:, cols])
    z_ref[...] = _dot(xn, wi_ref[:, 3 * DN_INNER:4 * DN_INNER])
    ba = _dot(xn, wi_ref[:, 4 * DN_INNER:])

    beta_all = _sigmoid(ba)
    ap = ba + gp_ref[1:2, :]
    softplus = jnp.maximum(ap, 0.0) + jnp.log(1.0 + jnp.exp(-jnp.abs(ap)))
    g_all = -jnp.exp(gp_ref[0:1, :]) * softplus
    pos = lax.broadcasted_iota(jnp.int32, (nt, 128), 0) % DN_CHUNK
    gc_all = g_all
    rs_all = g_all
    shift = 1
    while shift < DN_CHUNK:
        gc_all = gc_all + jnp.where(pos >= shift, pltpu.roll(gc_all, shift, axis=0), 0.0)
        rs_all = rs_all + jnp.where(pos < DN_CHUNK - shift, pltpu.roll(rs_all, nt - shift, axis=0), 0.0)
        shift *= 2
    rem_all = rs_all - g_all
    gc_rows = jnp.transpose(gc_all)

    side_row = lax.broadcasted_iota(jnp.int32, (ch, nt), 0)
    side_col = lax.broadcasted_iota(jnp.int32, (ch, nt), 1)
    col_chunk = side_col // ch
    causal_side = side_row >= side_col % ch
    strict_side = side_row > side_col % ch
    eye_side = (side_row == side_col % ch).astype(F32)

    def to_blockdiag(x_side):
        return jnp.concatenate([x_side.astype(BF16)] * nchunks, axis=0) * bd_ref[...]

    def fold(x):
        out = x[(nchunks - 1) * ch:]
        for c in reversed(range(nchunks - 1)):
            out = jnp.where(col_chunk == c, x[c * ch:(c + 1) * ch], out)
        return out

    def side_columns(col):
        out = jnp.broadcast_to(col[(nchunks - 1) * ch:], (ch, nt))
        for c in reversed(range(nchunks - 1)):
            out = jnp.where(col_chunk == c, col[c * ch:(c + 1) * ch], out)
        return out

    def conv_silu(col0):
        cols = slice(col0, col0 + DN_HEAD_DIM)
        acc = wc_ref[0:1, cols] * win_ref[5:5 + nt, cols]
        for j in range(1, DN_CONV):
            acc = acc + wc_ref[j:j + 1, cols] * win_ref[5 + j:5 + j + nt, cols]
        return acc * _sigmoid(acc)

    gcc, lower, attn, rhs, qd, kd = [], [], [], [], [], []
    for h in heads:
        beta = beta_all[:, h:h + 1]
        gcc.append(gc_all[:, DN_HEADS + h:DN_HEADS + h + 1])
        rem = rem_all[:, DN_HEADS + h:DN_HEADS + h + 1]
        gcr = gc_rows[DN_HEADS + h:DN_HEADS + h + 1, :]

        q = conv_silu(h * DN_HEAD_DIM)
        k = conv_silu(DN_INNER + h * DN_HEAD_DIM)
        v = conv_silu(2 * DN_INNER + h * DN_HEAD_DIM)
        q = q * (lax.rsqrt(jnp.sum(q * q, axis=-1, keepdims=True) + L2_EPS) * (DN_HEAD_DIM ** -0.5))
        k = k * lax.rsqrt(jnp.sum(k * k, axis=-1, keepdims=True) + L2_EPS)

        decay = jnp.exp(jnp.where(causal_side, side_columns(gcc[h]) - gcr, -jnp.inf))
        egc = jnp.exp(gcc[h])
        kb = k * beta
        both = _dot(jnp.concatenate([kb, q], axis=0).astype(BF16), jnp.transpose(k).astype(BF16))
        lower.append(jnp.where(strict_side, fold(both[:nt]) * decay, 0.0))
        attn.append(to_blockdiag(fold(both[nt:]) * decay))
        rhs.append(jnp.concatenate([v * beta, kb * egc], axis=1))
        qd.append((q * egc).astype(BF16))
        kd.append((k * jnp.exp(rem)).astype(BF16))

    inv = [eye_side - lower[h] for h in heads]
    power = [_dot(lower[h].astype(BF16), to_blockdiag(lower[h])) for h in heads]
    order = 2
    while order < ch:
        last = order * 2 >= ch
        for h in heads:
            wts = to_blockdiag(power[h])
            if last:
                inv[h] = inv[h] + _dot(inv[h].astype(BF16), wts)
            else:
                res = _dot(jnp.concatenate([inv[h], power[h]], axis=0).astype(BF16), wts)
                inv[h] = inv[h] + res[:ch]
                power[h] = res[ch:]
        order *= 2

    u, wbf = [], []
    for h in heads:
        sol = rhs[h] + _dot(to_blockdiag(inv[h] - eye_side), rhs[h].astype(BF16))
        u.append(sol[:, :DN_HEAD_DIM])
        wbf.append(sol[:, DN_HEAD_DIM:].astype(BF16))

    state = [state_ref[h] for h in heads]
    v_new = [[] for _ in heads]
    o_inter = [[] for _ in heads]
    for c in range(nchunks):
        rows = slice(c * ch, (c + 1) * ch)
        last_row = (c + 1) * ch - 1
        for h in heads:
            both = _dot(jnp.concatenate([wbf[h][rows], qd[h][rows]], axis=0), state[h].astype(BF16))
            vn = u[h][rows] - both[:ch]
            o_inter[h].append(both[ch:])
            v_new[h].append(vn)
            state[h] = (state[h] * jnp.exp(gcc[h][last_row:last_row + 1, :])
                        + _dot_tn(kd[h][rows], vn.astype(BF16)))

    for h in heads:
        state_ref[h] = state[h]
        vn = jnp.concatenate(v_new[h], axis=0).astype(BF16)
        o = jnp.concatenate(o_inter[h], axis=0) + _dot(attn[h], vn)
        o = o * lax.rsqrt(jnp.mean(o * o, axis=-1, keepdims=True) + RMS_EPS) * on_ref[...]
        zc = z_ref[:, h * DN_HEAD_DIM:(h + 1) * DN_HEAD_DIM]
        gated_ref[:, h * DN_HEAD_DIM:(h + 1) * DN_HEAD_DIM] = (o * (zc * _sigmoid(zc))).astype(BF16)

    o_ref[...] = x_ref[...] + _dot(gated_ref[...], wo_ref[...])
    win_ref[0:8, :] = win_ref[nt:nt + 8, :]


def deltanet(x, g, w_in, wc, gate_params, out_norm, w_out, *, batch, seq):
    tiles = seq // DN_TILE
    qkv_cols = 3 * DN_INNER
    d = x.shape[1]
    const = lambda b, i: (0, 0)
    chunk_of = jnp.arange(DN_TILE) // DN_CHUNK
    blockdiag_mask = (chunk_of[:, None] == chunk_of[None, :]).astype(BF16)
    return pl.pallas_call(
        _delta_kernel,
        grid=(batch, tiles),
        in_specs=[
            pl.BlockSpec((DN_TILE, d), lambda b, i: (b * tiles + i, 0)),
            pl.BlockSpec((1, d), const),
            pl.BlockSpec((d, DN_PROJ_COLS), const),
            pl.BlockSpec((DN_CONV, qkv_cols), const),
            pl.BlockSpec((8, 128), const),
            pl.BlockSpec((1, DN_HEAD_DIM), const),
            pl.BlockSpec((DN_INNER, d), const),
            pl.BlockSpec((DN_TILE, DN_TILE), const),
        ],
        out_specs=pl.BlockSpec((DN_TILE, d), lambda b, i: (b * tiles + i, 0)),
        out_shape=jax.ShapeDtypeStruct((batch * seq, d), F32),
        scratch_shapes=[
            pltpu.VMEM((DN_TILE + 8, qkv_cols), F32),
            pltpu.VMEM((DN_TILE, DN_INNER), F32),
            pltpu.VMEM((DN_HEADS, DN_HEAD_DIM, DN_HEAD_DIM), F32),
            pltpu.VMEM((DN_TILE, DN_INNER), BF16),
        ],
        compiler_params=_params("parallel", "arbitrary"),
        name="deltanet",
    )(x, g.reshape(1, d), w_in, wc, gate_params, out_norm.reshape(1, DN_HEAD_DIM), w_out, blockdiag_mask)


ROW_TILE = 1024
XA_TILE = 512
CV_TILE = 512


def _deltanet_layer(h2d, norm, w_in, w_conv, a_log, dt_bias, out_norm, w_out, *, batch, seq):
    w_in_p = jnp.pad(w_in, ((0, 0), (0, DN_PROJ_COLS - w_in.shape[1]))).astype(BF16)
    gate_params = jnp.zeros((8, 128), F32)
    gate_params = gate_params.at[0, DN_HEADS:2 * DN_HEADS].set(a_log)
    gate_params = gate_params.at[1, DN_HEADS:2 * DN_HEADS].set(dt_bias)
    return deltanet(h2d, norm, w_in_p, w_conv, gate_params, out_norm, w_out.astype(BF16),
                    batch=batch, seq=seq)


def _conformer_layer(h2d, norm, w_pw1, b_pw1, w_dw, b_dw, ln_g, ln_b, w_pw2, b_pw2, *, batch, seq):
    w1 = w_pw1.astype(BF16)
    out = conformer(h2d.reshape(batch, seq, D_MODEL), norm, w1[:, :D_MODEL], w1[:, D_MODEL:],
                    b_pw1[:D_MODEL], b_pw1[D_MODEL:], w_dw, b_dw, ln_g, ln_b, w_pw2.astype(BF16), b_pw2,
                    ts=CV_TILE)
    return out.reshape(batch * seq, D_MODEL)


def _xattn_layer(h2d, mem2d, norm, mem_norm, w_q, w_kv, w_o, *, batch, seq, name):
    mlen = mem2d.shape[0] // batch
    kv = rms_matmul(mem2d, mem_norm, w_kv.astype(BF16), tm=ROW_TILE, tn=ROW_TILE, out_dtype=BF16,
                    name=name + "_kv")
    out = xattn(h2d.reshape(batch, seq, D_MODEL), norm, w_q.astype(BF16),
                kv.reshape(batch, mlen, 2 * D_MODEL), w_o.astype(BF16), tm=XA_TILE, name=name)
    return out.reshape(batch * seq, D_MODEL)


def kernel(x, mem, dn_norm, dn_w_in, dn_w_conv, dn_a_log, dn_dt_bias, dn_out_norm, dn_w_out, cv_norm, cv_w_pw1, cv_b_pw1, cv_w_dw, cv_b_dw, cv_ln_g, cv_ln_b, cv_w_pw2, cv_b_pw2, xa_norm, xa_mem_norm, xa_w_q, xa_w_kv, xa_w_o, mlp_norm, mlp_w_up, mlp_w_down, final_norm):
    batch, seq, d = x.shape
    depth = xa_norm.shape[0]
    h = x.reshape(batch * seq, d)
    mem2d = mem.reshape(batch * mem.shape[1], d)
    for layer in range(depth):
        j = layer // 2
        if layer % 2 == 0:
            h = _deltanet_layer(h, dn_norm[j], dn_w_in[j], dn_w_conv[j], dn_a_log[j], dn_dt_bias[j],
                                dn_out_norm[j], dn_w_out[j], batch=batch, seq=seq)
        else:
            h = _conformer_layer(h, cv_norm[j], cv_w_pw1[j], cv_b_pw1[j], cv_w_dw[j], cv_b_dw[j],
                                 cv_ln_g[j], cv_ln_b[j], cv_w_pw2[j], cv_b_pw2[j], batch=batch, seq=seq)
        h = _xattn_layer(h, mem2d, xa_norm[layer], xa_mem_norm[layer], xa_w_q[layer], xa_w_kv[layer],
                         xa_w_o[layer], batch=batch, seq=seq, name=f"xattn{layer}")
        last = layer == depth - 1
        h = mlp(h, mlp_norm[layer], mlp_w_up[layer].astype(BF16), mlp_w_down[layer].astype(BF16),
                final_norm, tm=ROW_TILE, tf=1024, final_norm=last, name=f"mlp{layer}")
    return h.reshape(batch, seq, d)
```

```python
import functools

import jax
import jax.numpy as jnp
from jax import lax
from jax.experimental import pallas as pl
from jax.experimental.pallas import tpu as pltpu

F32 = jnp.float32
BF16 = jnp.bfloat16

D_MODEL = 1024
DN_HEADS = 8
DN_HEAD_DIM = 128
DN_INNER = DN_HEADS * DN_HEAD_DIM
DN_CONV = 4
DN_CHUNK = 64
DN_PROJ_COLS = 4 * DN_INNER + 128
CV_WIDTH = 31
CV_HALO = 32
CV_ROWS = 128
XA_HEADS = 4
XA_HEAD_DIM = D_MODEL // XA_HEADS
FF_CHUNK = 1024
RMS_EPS = 1e-6
LN_EPS = 1e-5
L2_EPS = 1e-6

DN_TILE = 256
CV_TILE = 512
KV_TILE = 1024
VMEM_LIMIT_BYTES = 60 * 1024 * 1024


def _params(*semantics):
    return pltpu.CompilerParams(dimension_semantics=semantics, vmem_limit_bytes=VMEM_LIMIT_BYTES)


def _sigmoid(x):
    return 0.5 * jnp.tanh(0.5 * x) + 0.5


def _rms_norm(x, g):
    ms = jnp.mean(x * x, axis=-1, keepdims=True)
    return x * lax.rsqrt(ms + RMS_EPS) * g


def _dot(a, b):
    return jnp.dot(a, b, preferred_element_type=F32)


def _dot_nt(a, b):
    return lax.dot_general(a, b, (((1,), (1,)), ((), ())), preferred_element_type=F32)


def _dot_tn(a, b):
    return lax.dot_general(a, b, (((0,), (0,)), ((), ())), preferred_element_type=F32)


def _rms_matmul_kernel(x_ref, g_ref, w_ref, o_ref, xn_ref):
    @pl.when(pl.program_id(1) == 0)
    def _():
        xn_ref[...] = _rms_norm(x_ref[...], g_ref[...]).astype(BF16)

    o_ref[...] = _dot(xn_ref[...], w_ref[...]).astype(o_ref.dtype)


def rms_matmul(x, g, w, *, tm, tn, out_dtype, name):
    m, k = x.shape
    n = w.shape[1]
    return pl.pallas_call(
        _rms_matmul_kernel,
        grid=(m // tm, n // tn),
        in_specs=[
            pl.BlockSpec((tm, k), lambda i, j: (i, 0)),
            pl.BlockSpec((1, k), lambda i, j: (0, 0)),
            pl.BlockSpec((k, tn), lambda i, j: (0, j)),
        ],
        out_specs=pl.BlockSpec((tm, tn), lambda i, j: (i, j)),
        out_shape=jax.ShapeDtypeStruct((m, n), out_dtype),
        scratch_shapes=[pltpu.VMEM((tm, k), BF16)],
        compiler_params=_params("parallel", "arbitrary"),
        name=name,
    )(x, g.reshape(1, k), w)


class _AttnMlp:
    def __init__(self, h, gx_ref, wq_ref, k_ref, v_ref, wo_ref, gm_ref, wu_ref, wd_ref):
        self.h = h
        self.refs = (gx_ref, wq_ref, k_ref, v_ref, wo_ref, gm_ref, wu_ref, wd_ref)
        self.num_chunks = wu_ref.shape[1] // FF_CHUNK

    def attention(self):
        gx_ref, wq_ref, k_ref, v_ref, wo_ref, gm_ref, _, _ = self.refs
        xn = _rms_norm(self.h, gx_ref[...]).astype(BF16)
        q = (_dot(xn, wq_ref[...]) * (XA_HEAD_DIM ** -0.5)).astype(BF16)
        outs = []
        for hd in range(XA_HEADS):
            cols = slice(hd * XA_HEAD_DIM, (hd + 1) * XA_HEAD_DIM)
            s = _dot_nt(q[:, cols], k_ref[0, :, cols])
            p = jnp.exp(s - jnp.max(s, axis=-1, keepdims=True))
            denom = jnp.sum(p, axis=-1, keepdims=True)
            outs.append((_dot(p.astype(BF16), v_ref[0, :, cols]) / denom).astype(BF16))
        self.acc = self.h + _dot(jnp.concatenate(outs, axis=-1), wo_ref[...])
        self.xm = _rms_norm(self.acc, gm_ref[...]).astype(BF16)
        return self.acc[0:1, 0:128]

    def mlp_chunk(self, i):
        wu_ref, wd_ref = self.refs[6:]
        cols = slice(i * FF_CHUNK, (i + 1) * FF_CHUNK)
        hid = _dot(self.xm, wu_ref[:, cols])
        hid = jnp.square(jnp.maximum(hid, 0.0)).astype(BF16)
        self.acc = self.acc + _dot(hid, wd_ref[cols, :])
        return self.acc[0:1, 0:128]


def _run_lagged(s, tiles, lag_ref, lag_refs, gf_ref, o_ref, mixer, *, final_norm):
    @pl.when(s == 0)
    def _():
        lag_ref[1] = jnp.zeros(lag_ref.shape[1:], F32)

    stream = _AttnMlp(lag_ref[(s + 1) % 2], *lag_refs)
    stages = [stream.attention] + [functools.partial(stream.mlp_chunk, i) for i in range(stream.num_chunks)]
    lag_ref[s % 2] = mixer(stages)
    for stage in stages:
        stage()
    out = stream.acc
    o_ref[...] = _rms_norm(out, gf_ref[...]) if final_norm else out


def _take(stages, n):
    token = None
    for _ in range(min(n, len(stages))):
        token = stages.pop(0)()
    if token is None:
        return jnp.zeros((1, 128), F32)
    bits = lax.bitcast_convert_type(token, jnp.uint32)
    bits = lax.shift_right_logical(lax.shift_right_logical(bits, jnp.uint32(16)), jnp.uint32(16))
    return lax.bitcast_convert_type(bits, F32)


def _conformer_mixer(x, first, refs, win_ref, c_ref, stages):
    g_ref, w1a_ref, w1b_ref, b1a_ref, b1b_ref, wdw_ref, bdw_ref, lg_ref, lb_ref, w2_ref, b2_ref = refs
    ts = x.shape[0]

    @pl.when(first)
    def _():
        win_ref[0:CV_HALO, :] = jnp.zeros((CV_HALO, D_MODEL), F32)

    xn = _rms_norm(x, g_ref[...]).astype(BF16)
    gate = _dot(xn, w1b_ref[...]) + b1b_ref[...]
    win_ref[CV_HALO:, :] = (_dot(xn, w1a_ref[...]) + b1a_ref[...]) * _sigmoid(gate)
    base = CV_HALO - (CV_WIDTH - 1)
    after = None
    for cs in range(D_MODEL // 128):
        cols = slice(cs * 128, (cs + 1) * 128)
        taps = wdw_ref[:, cols] if after is None else wdw_ref[:, cols] + after
        for r0 in range(0, ts, CV_ROWS):
            partial = {}
            for j in range(CV_WIDTH):
                start = r0 + base + j
                term = taps[j:j + 1] * win_ref[start:start + CV_ROWS, cols]
                phase = start % 8
                partial[phase] = partial[phase] + term if phase in partial else term
            acc = partial[0] + bdw_ref[0:1, cols]
            for phase in range(1, 8):
                acc = acc + partial[phase]
            c_ref[r0:r0 + CV_ROWS, cols] = acc
        if cs % 2 == 1:
            after = _take(stages, 1)

    win_ref[0:CV_HALO, :] = win_ref[ts:ts + CV_HALO, :]

    c = c_ref[...]
    mu = jnp.mean(c, axis=-1, keepdims=True)
    xc = c - mu
    gain = lg_ref[...] + jnp.tile(after, (1, D_MODEL // 128))
    y = xc * lax.rsqrt(jnp.mean(xc * xc, axis=-1, keepdims=True) + LN_EPS) * gain + lb_ref[...]
    y = (y * _sigmoid(y)).astype(BF16)
    return x + _dot(y, w2_ref[...]) + b2_ref[...]


def _conformer_layer_kernel(*refs, tiles, final_norm):
    x_ref = refs[0]
    mixer_refs = refs[1:12]
    lag_refs = refs[12:20]
    gf_ref, o_ref, win_ref, c_ref, lag_ref = refs[20:]
    s = pl.program_id(0)
    mixer = functools.partial(_conformer_mixer, x_ref[...], s % tiles == 0, mixer_refs, win_ref, c_ref)
    _run_lagged(s, tiles, lag_ref, lag_refs, gf_ref, o_ref, mixer, final_norm=final_norm)


def _delta_mixer(x, first, refs, win_ref, z_ref, state_ref, gated_ref, stages):
    g_ref, wi_ref, wc_ref, gp_ref, on_ref, wo_ref, bd_ref = refs
    nt = x.shape[0]
    ch = DN_CHUNK
    nchunks = nt // ch
    heads = range(DN_HEADS)

    @pl.when(first)
    def _():
        win_ref[0:8, :] = jnp.zeros((8, 3 * DN_INNER), F32)
        state_ref[...] = jnp.zeros(state_ref.shape, F32)

    xn = _rms_norm(x, g_ref[...]).astype(BF16)
    for part in range(3):
        cols = slice(part * DN_INNER, (part + 1) * DN_INNER)
        win_ref[8:8 + nt, cols] = _dot(xn, wi_ref[:, cols])
    z_ref[...] = _dot(xn, wi_ref[:, 3 * DN_INNER:4 * DN_INNER])
    ba = _dot(xn, wi_ref[:, 4 * DN_INNER:])
    _take(stages, 1)

    beta_all = _sigmoid(ba)
    ap = ba + gp_ref[1:2, :]
    softplus = jnp.maximum(ap, 0.0) + jnp.log(1.0 + jnp.exp(-jnp.abs(ap)))
    g_all = -jnp.exp(gp_ref[0:1, :]) * softplus
    pos = lax.broadcasted_iota(jnp.int32, (nt, 128), 0) % ch
    gc_all = g_all
    rs_all = g_all
    shift = 1
    while shift < ch:
        gc_all = gc_all + jnp.where(pos >= shift, pltpu.roll(gc_all, shift, axis=0), 0.0)
        rs_all = rs_all + jnp.where(pos < ch - shift, pltpu.roll(rs_all, nt - shift, axis=0), 0.0)
        shift *= 2
    rem_all = rs_all - g_all
    gc_rows = jnp.transpose(gc_all)

    side_row = lax.broadcasted_iota(jnp.int32, (ch, nt), 0)
    side_col = lax.broadcasted_iota(jnp.int32, (ch, nt), 1)
    col_chunk = side_col // ch
    causal_side = side_row >= side_col % ch
    strict_side = side_row > side_col % ch
    eye_side = (side_row == side_col % ch).astype(F32)

    def to_blockdiag(x_side):
        return jnp.concatenate([x_side.astype(BF16)] * nchunks, axis=0) * bd_ref[...]

    def fold(m):
        out = m[(nchunks - 1) * ch:]
        for c in reversed(range(nchunks - 1)):
            out = jnp.where(col_chunk == c, m[c * ch:(c + 1) * ch], out)
        return out

    def side_columns(col):
        out = jnp.broadcast_to(col[(nchunks - 1) * ch:], (ch, nt))
        for c in reversed(range(nchunks - 1)):
            out = jnp.where(col_chunk == c, col[c * ch:(c + 1) * ch], out)
        return out

    def conv_silu(col0):
        cols = slice(col0, col0 + DN_HEAD_DIM)
        acc = wc_ref[0:1, cols] * win_ref[5:5 + nt, cols]
        for j in range(1, DN_CONV):
            acc = acc + wc_ref[j:j + 1, cols] * win_ref[5 + j:5 + j + nt, cols]
        return acc * _sigmoid(acc)

    gcc, lower, attn, rhs, qd, kd = [], [], [], [], [], []
    for h in heads:
        beta = beta_all[:, h:h + 1]
        gcc.append(gc_all[:, DN_HEADS + h:DN_HEADS + h + 1])
        rem = rem_all[:, DN_HEADS + h:DN_HEADS + h + 1]
        gcr = gc_rows[DN_HEADS + h:DN_HEADS + h + 1, :]

        q = conv_silu(h * DN_HEAD_DIM)
        k = conv_silu(DN_INNER + h * DN_HEAD_DIM)
        v = conv_silu(2 * DN_INNER + h * DN_HEAD_DIM)
        q = q * (lax.rsqrt(jnp.sum(q * q, axis=-1, keepdims=True) + L2_EPS) * (DN_HEAD_DIM ** -0.5))
        k = k * lax.rsqrt(jnp.sum(k * k, axis=-1, keepdims=True) + L2_EPS)

        decay = jnp.exp(jnp.where(causal_side, side_columns(gcc[h]) - gcr, -jnp.inf))
        egc = jnp.exp(gcc[h])
        kb = k * beta
        both = _dot(jnp.concatenate([kb, q], axis=0).astype(BF16), jnp.transpose(k).astype(BF16))
        lower.append(jnp.where(strict_side, fold(both[:nt]) * decay, 0.0))
        attn.append(to_blockdiag(fold(both[nt:]) * decay))
        rhs.append(jnp.concatenate([v * beta, kb * egc], axis=1))
        qd.append((q * egc).astype(BF16))
        kd.append((k * jnp.exp(rem)).astype(BF16))
        if h % 4 == 3:
            _take(stages, 1)

    inv = [eye_side - lower[h] for h in heads]
    power = [_dot(lower[h].astype(BF16), to_blockdiag(lower[h])) for h in heads]
    order = 2
    while order < ch:
        last = order * 2 >= ch
        for h in heads:
            wts = to_blockdiag(power[h])
            if last:
                inv[h] = inv[h] + _dot(inv[h].astype(BF16), wts)
            else:
                res = _dot(jnp.concatenate([inv[h], power[h]], axis=0).astype(BF16), wts)
                inv[h] = inv[h] + res[:ch]
                power[h] = res[ch:]
        order *= 2
    _take(stages, 1)

    u, wbf = [], []
    for h in heads:
        sol = rhs[h] + _dot(to_blockdiag(inv[h] - eye_side), rhs[h].astype(BF16))
        u.append(sol[:, :DN_HEAD_DIM])
        wbf.append(sol[:, DN_HEAD_DIM:].astype(BF16))
    _take(stages, 1)

    state = [state_ref[h] for h in heads]
    v_new = [[] for _ in heads]
    o_inter = [[] for _ in heads]
    for c in range(nchunks):
        rows = slice(c * ch, (c + 1) * ch)
        last_row = (c + 1) * ch - 1
        for h in heads:
            both = _dot(jnp.concatenate([wbf[h][rows], qd[h][rows]], axis=0), state[h].astype(BF16))
            vn = u[h][rows] - both[:ch]
            o_inter[h].append(both[ch:])
            v_new[h].append(vn)
            state[h] = (state[h] * jnp.exp(gcc[h][last_row:last_row + 1, :])
                        + _dot_tn(kd[h][rows], vn.astype(BF16)))

    for h in heads:
        state_ref[h] = state[h]
        vn = jnp.concatenate(v_new[h], axis=0).astype(BF16)
        o = jnp.concatenate(o_inter[h], axis=0) + _dot(attn[h], vn)
        o = o * lax.rsqrt(jnp.mean(o * o, axis=-1, keepdims=True) + RMS_EPS) * on_ref[...]
        zc = z_ref[:, h * DN_HEAD_DIM:(h + 1) * DN_HEAD_DIM]
        gated_ref[:, h * DN_HEAD_DIM:(h + 1) * DN_HEAD_DIM] = (o * (zc * _sigmoid(zc))).astype(BF16)

    win_ref[0:8, :] = win_ref[nt:nt + 8, :]
    return x + _dot(gated_ref[...], wo_ref[...])


def _deltanet_layer_kernel(*refs, tiles, final_norm):
    x_ref = refs[0]
    mixer_refs = refs[1:8]
    lag_refs = refs[8:16]
    gf_ref, o_ref, win_ref, z_ref, state_ref, gated_ref, lag_ref = refs[16:]
    s = pl.program_id(0)
    mixer = functools.partial(_delta_mixer, x_ref[...], s % tiles == 0, mixer_refs,
                              win_ref, z_ref, state_ref, gated_ref)
    _run_lagged(s, tiles, lag_ref, lag_refs, gf_ref, o_ref, mixer, final_norm=final_norm)


def _layer_call(body, h2d, mixer_args, lag_args, kv, gf, scratch, *, tile, batch, seq, final_norm, name):
    tiles = seq // tile
    steps = batch * tiles
    d = h2d.shape[1]
    mlen = kv.shape[1]
    gx, wq, wo, gm, wu, wd = lag_args

    def const(a):
        return pl.BlockSpec(a.shape, lambda s: (0,) * a.ndim)

    row = lambda v: v.reshape(1, -1)
    lag_operands = [row(gx), wq, kv, kv, wo, row(gm), wu, wd]
    lag_batch = lambda s: jnp.maximum(s - 1, 0) // tiles
    lag_specs = [
        const(lag_operands[0]), const(wq),
        pl.BlockSpec((1, mlen, d), lambda s: (lag_batch(s), 0, 0)),
        pl.BlockSpec((1, mlen, d), lambda s: (lag_batch(s), 0, 1)),
        const(wo), const(lag_operands[5]), const(wu), const(wd),
    ]
    return pl.pallas_call(
        functools.partial(body, tiles=tiles, final_norm=final_norm),
        grid=(steps + 1,),
        in_specs=[pl.BlockSpec((tile, d), lambda s: (jnp.minimum(s, steps - 1), 0))]
                 + [const(a) for a in mixer_args] + lag_specs + [const(row(gf))],
        out_specs=pl.BlockSpec((tile, d), lambda s: (jnp.maximum(s - 1, 0), 0)),
        out_shape=jax.ShapeDtypeStruct(h2d.shape, F32),
        scratch_shapes=scratch + [pltpu.VMEM((2, tile, d), F32)],
        compiler_params=_params("arbitrary"),
        name=name,
    )(h2d, *mixer_args, *lag_operands, row(gf))


def kernel(x, mem, dn_norm, dn_w_in, dn_w_conv, dn_a_log, dn_dt_bias, dn_out_norm, dn_w_out, cv_norm, cv_w_pw1, cv_b_pw1, cv_w_dw, cv_b_dw, cv_ln_g, cv_ln_b, cv_w_pw2, cv_b_pw2, xa_norm, xa_mem_norm, xa_w_q, xa_w_kv, xa_w_o, mlp_norm, mlp_w_up, mlp_w_down, final_norm):
    batch, seq, d = x.shape
    depth = xa_norm.shape[0]
    mlen = mem.shape[1]
    h = x.reshape(batch * seq, d)
    mem2d = mem.reshape(batch * mlen, d)
    row = lambda v: v.reshape(1, -1)
    chunk_of = jnp.arange(DN_TILE) // DN_CHUNK
    blockdiag_mask = (chunk_of[:, None] == chunk_of[None, :]).astype(BF16)
    for layer in range(depth):
        j = layer // 2
        kv = rms_matmul(mem2d, xa_mem_norm[layer], xa_w_kv[layer].astype(BF16), tm=KV_TILE, tn=KV_TILE,
                        out_dtype=BF16, name=f"kv{layer}").reshape(batch, mlen, 2 * d)
        lag_args = (xa_norm[layer], xa_w_q[layer].astype(BF16), xa_w_o[layer].astype(BF16),
                    mlp_norm[layer], mlp_w_up[layer].astype(BF16), mlp_w_down[layer].astype(BF16))
        common = dict(batch=batch, seq=seq, final_norm=layer == depth - 1)
        if layer % 2 == 0:
            w_in = jnp.pad(dn_w_in[j], ((0, 0), (0, DN_PROJ_COLS - dn_w_in.shape[2]))).astype(BF16)
            gate_params = jnp.zeros((8, 128), F32)
            gate_params = gate_params.at[0, DN_HEADS:2 * DN_HEADS].set(dn_a_log[j])
            gate_params = gate_params.at[1, DN_HEADS:2 * DN_HEADS].set(dn_dt_bias[j])
            mixer_args = (row(dn_norm[j]), w_in, dn_w_conv[j], gate_params, row(dn_out_norm[j]),
                          dn_w_out[j].astype(BF16), blockdiag_mask)
            scratch = [
                pltpu.VMEM((DN_TILE + 8, 3 * DN_INNER), F32),
                pltpu.VMEM((DN_TILE, DN_INNER), F32),
                pltpu.VMEM((DN_HEADS, DN_HEAD_DIM, DN_HEAD_DIM), F32),
                pltpu.VMEM((DN_TILE, DN_INNER), BF16),
            ]
            h = _layer_call(_deltanet_layer_kernel, h, mixer_args, lag_args, kv, final_norm, scratch,
                            tile=DN_TILE, name=f"deltanet_layer{layer}", **common)
        else:
            w1 = cv_w_pw1[j].astype(BF16)
            mixer_args = (row(cv_norm[j]), w1[:, :d], w1[:, d:], row(cv_b_pw1[j, :d]), row(cv_b_pw1[j, d:]),
                          cv_w_dw[j], row(cv_b_dw[j]), row(cv_ln_g[j]), row(cv_ln_b[j]),
                          cv_w_pw2[j].astype(BF16), row(cv_b_pw2[j]))
            scratch = [pltpu.VMEM((CV_TILE + CV_HALO, d), F32), pltpu.VMEM((CV_TILE, d), F32)]
            h = _layer_call(_conformer_layer_kernel, h, mixer_args, lag_args, kv, final_norm, scratch,
                            tile=CV_TILE, name=f"conformer_layer{layer}", **common)
    return h.reshape(batch, seq, d)
```

```python
import functools

import jax
import jax.numpy as jnp
from jax import lax
from jax.experimental import pallas as pl
from jax.experimental.pallas import tpu as pltpu

F32 = jnp.float32
BF16 = jnp.bfloat16

D_MODEL = 1024
DN_HEADS = 8
DN_HEAD_DIM = 128
DN_INNER = DN_HEADS * DN_HEAD_DIM
DN_CONV = 4
DN_CHUNK = 64
DN_PROJ_COLS = 4 * DN_INNER + 128
DN_WIN_COLS = 256
CV_WIDTH = 31
CV_HALO = 32
CV_ROWS = 128
XA_HEADS = 4
XA_HEAD_DIM = D_MODEL // XA_HEADS
DN_FF_CHUNK = 1024
CV_FF_CHUNK = 512
RMS_EPS = 1e-6
LN_EPS = 1e-5
L2_EPS = 1e-6

DN_TILE = 256
CV_TILE = 512
KV_TILE = 1024
VMEM_LIMIT_BYTES = 60 * 1024 * 1024


def _params(*semantics):
    return pltpu.CompilerParams(dimension_semantics=semantics, vmem_limit_bytes=VMEM_LIMIT_BYTES)


def _sigmoid(x):
    return 0.5 * jnp.tanh(0.5 * x) + 0.5


def _rms_norm(x, g):
    ms = jnp.mean(x * x, axis=-1, keepdims=True)
    return x * lax.rsqrt(ms + RMS_EPS) * g


def _dot(a, b):
    return jnp.dot(a, b, preferred_element_type=F32)


def _dot_nt(a, b):
    return lax.dot_general(a, b, (((1,), (1,)), ((), ())), preferred_element_type=F32)


def _dot_tn(a, b):
    return lax.dot_general(a, b, (((0,), (0,)), ((), ())), preferred_element_type=F32)


def _rms_matmul_kernel(x_ref, g_ref, w_ref, o_ref, xn_ref):
    @pl.when(pl.program_id(1) == 0)
    def _():
        xn_ref[...] = _rms_norm(x_ref[...], g_ref[...]).astype(BF16)

    o_ref[...] = _dot(xn_ref[...], w_ref[...]).astype(o_ref.dtype)


def rms_matmul(x, g, w, *, tm, tn, out_dtype, name):
    m, k = x.shape
    n = w.shape[1]
    return pl.pallas_call(
        _rms_matmul_kernel,
        grid=(m // tm, n // tn),
        in_specs=[
            pl.BlockSpec((tm, k), lambda i, j: (i, 0)),
            pl.BlockSpec((1, k), lambda i, j: (0, 0)),
            pl.BlockSpec((k, tn), lambda i, j: (0, j)),
        ],
        out_specs=pl.BlockSpec((tm, tn), lambda i, j: (i, j)),
        out_shape=jax.ShapeDtypeStruct((m, n), out_dtype),
        scratch_shapes=[pltpu.VMEM((tm, k), BF16)],
        compiler_params=_params("parallel", "arbitrary"),
        name=name,
    )(x, g.reshape(1, k), w)


class _AttnMlp:
    def __init__(self, h, ff_chunk, gx_ref, wq_ref, k_ref, v_ref, wo_ref, gm_ref, wu_ref, wd_ref):
        self.h = h
        self.refs = (gx_ref, wq_ref, k_ref, v_ref, wo_ref, gm_ref, wu_ref, wd_ref)
        self.ff_chunk = ff_chunk
        self.num_chunks = wu_ref.shape[1] // ff_chunk

    def attention(self):
        gx_ref, wq_ref, k_ref, v_ref, wo_ref, gm_ref, _, _ = self.refs
        xn = _rms_norm(self.h, gx_ref[...]).astype(BF16)
        q = (_dot(xn, wq_ref[...]) * (XA_HEAD_DIM ** -0.5)).astype(BF16)
        outs = []
        for hd in range(XA_HEADS):
            cols = slice(hd * XA_HEAD_DIM, (hd + 1) * XA_HEAD_DIM)
            s = _dot_nt(q[:, cols], k_ref[0, :, cols])
            p = jnp.exp(s - jnp.max(s, axis=-1, keepdims=True))
            denom = jnp.sum(p, axis=-1, keepdims=True)
            outs.append((_dot(p.astype(BF16), v_ref[0, :, cols]) / denom).astype(BF16))
        self.acc = self.h + _dot(jnp.concatenate(outs, axis=-1), wo_ref[...])
        self.xm = _rms_norm(self.acc, gm_ref[...]).astype(BF16)
        self.hid = self._up(0)
        return self.acc[0:1, 0:128]

    def _up(self, i):
        wu_ref = self.refs[6]
        hid = _dot(self.xm, wu_ref[:, i * self.ff_chunk:(i + 1) * self.ff_chunk])
        return jnp.square(jnp.maximum(hid, 0.0)).astype(BF16)

    def mlp_chunk(self, i):
        wd_ref = self.refs[7]
        hid = self.hid
        if i + 1 < self.num_chunks:
            self.hid = self._up(i + 1)
        self.acc = self.acc + _dot(hid, wd_ref[i * self.ff_chunk:(i + 1) * self.ff_chunk, :])
        return self.acc[0:1, 0:128]


def _run_lagged(s, tiles, lag_ref, lag_refs, gf_ref, o_ref, mixer, *, final_norm, ff_chunk):
    @pl.when(s == 0)
    def _():
        lag_ref[1] = jnp.zeros(lag_ref.shape[1:], F32)

    stream = _AttnMlp(lag_ref[(s + 1) % 2], ff_chunk, *lag_refs)
    stages = [stream.attention] + [functools.partial(stream.mlp_chunk, i) for i in range(stream.num_chunks)]
    lag_ref[s % 2] = mixer(stages)
    for stage in stages:
        stage()
    out = stream.acc
    o_ref[...] = _rms_norm(out, gf_ref[...]) if final_norm else out


def _take(stages, n):
    token = None
    for _ in range(min(n, len(stages))):
        token = stages.pop(0)()
    if token is None:
        return jnp.zeros((1, 128), F32)
    bits = lax.bitcast_convert_type(token, jnp.uint32)
    bits = lax.shift_right_logical(lax.shift_right_logical(bits, jnp.uint32(16)), jnp.uint32(16))
    return lax.bitcast_convert_type(bits, F32)


def _conformer_mixer(x, first, refs, win_ref, c_ref, stages):
    g_ref, w1_ref, b1_ref, wdw_ref, bdw_ref, lg_ref, lb_ref, w2_ref, b2_ref = refs
    ts = x.shape[0]

    @pl.when(first)
    def _():
        win_ref[0:CV_HALO, :] = jnp.zeros((CV_HALO, D_MODEL), F32)

    xn = _rms_norm(x, g_ref[...]).astype(BF16)
    gate = _dot(xn, w1_ref[:, D_MODEL:]) + b1_ref[:, D_MODEL:]
    win_ref[CV_HALO:, :] = (_dot(xn, w1_ref[:, :D_MODEL]) + b1_ref[:, :D_MODEL]) * _sigmoid(gate)
    base = CV_HALO - (CV_WIDTH - 1)
    done = []
    for cs in range(D_MODEL // 128):
        cols = slice(cs * 128, (cs + 1) * 128)
        taps = wdw_ref[:, cols] if cs < 1 else wdw_ref[:, cols] + done[cs - 1]
        for r0 in range(0, ts, CV_ROWS):
            partial = {}
            for j in range(CV_WIDTH):
                start = r0 + base + j
                term = taps[j:j + 1] * win_ref[start:start + CV_ROWS, cols]
                phase = start % 8
                partial[phase] = partial[phase] + term if phase in partial else term
            acc = partial[0] + bdw_ref[0:1, cols]
            for phase in range(1, 8):
                acc = acc + partial[phase]
            c_ref[r0:r0 + CV_ROWS, cols] = acc
        done.append(_take(stages, 1))

    win_ref[0:CV_HALO, :] = win_ref[ts:ts + CV_HALO, :]

    c = c_ref[...]
    mu = jnp.mean(c, axis=-1, keepdims=True)
    xc = c - mu
    gain = lg_ref[...] + jnp.tile(done[-1], (1, D_MODEL // 128))
    y = xc * lax.rsqrt(jnp.mean(xc * xc, axis=-1, keepdims=True) + LN_EPS) * gain + lb_ref[...]
    y = (y * _sigmoid(y)).astype(BF16)
    return x + _dot(y, w2_ref[...]) + b2_ref[...]


def _conformer_layer_kernel(*refs, tiles, final_norm):
    x_ref = refs[0]
    mixer_refs = refs[1:10]
    lag_refs = refs[10:18]
    gf_ref, o_ref, win_ref, c_ref, lag_ref = refs[18:]
    s = pl.program_id(0)
    mixer = functools.partial(_conformer_mixer, x_ref[...], s % tiles == 0, mixer_refs, win_ref, c_ref)
    _run_lagged(s, tiles, lag_ref, lag_refs, gf_ref, o_ref, mixer, final_norm=final_norm,
                ff_chunk=CV_FF_CHUNK)


def _delta_mixer(x, first, refs, wins, z_ref, state_ref, gated_ref, stages):
    g_ref, wi_ref, wc_ref, gp_ref, on_ref, wo_ref, bd_ref = refs
    nt = x.shape[0]
    ch = DN_CHUNK
    nchunks = nt // ch
    heads = range(DN_HEADS)

    @pl.when(first)
    def _():
        for win_ref in wins:
            win_ref[0:8, :] = jnp.zeros((8, DN_WIN_COLS), F32)
        state_ref[...] = jnp.zeros(state_ref.shape, F32)

    xn = _rms_norm(x, g_ref[...]).astype(BF16)
    ba = _dot(xn, wi_ref[:, 4 * DN_INNER:])
    for pair in range(DN_INNER // DN_WIN_COLS):
        for part in range(3):
            col0 = part * DN_INNER + pair * DN_WIN_COLS
            wins[3 * pair + part][8:8 + nt, :] = _dot(xn, wi_ref[:, col0:col0 + DN_WIN_COLS])
    z_ref[...] = _dot(xn, wi_ref[:, 3 * DN_INNER:4 * DN_INNER])
    _take(stages, 1)

    beta_all = _sigmoid(ba)
    ap = ba + gp_ref[1:2, :]
    softplus = jnp.maximum(ap, 0.0) + jnp.log(1.0 + jnp.exp(-jnp.abs(ap)))
    g_all = -jnp.exp(gp_ref[0:1, :]) * softplus
    pos = lax.broadcasted_iota(jnp.int32, (nt, 128), 0) % ch
    gc_all = g_all
    rs_all = g_all
    shift = 1
    while shift < ch:
        gc_all = gc_all + jnp.where(pos >= shift, pltpu.roll(gc_all, shift, axis=0), 0.0)
        rs_all = rs_all + jnp.where(pos < ch - shift, pltpu.roll(rs_all, nt - shift, axis=0), 0.0)
        shift *= 2
    rem_all = rs_all - g_all
    gc_rows = jnp.transpose(gc_all)

    side_row = lax.broadcasted_iota(jnp.int32, (ch, nt), 0)
    side_col = lax.broadcasted_iota(jnp.int32, (ch, nt), 1)
    col_chunk = side_col // ch
    causal_side = side_row >= side_col % ch
    strict_side = side_row > side_col % ch
    eye_side = (side_row == side_col % ch).astype(F32)

    def to_blockdiag(x_side):
        return jnp.concatenate([x_side.astype(BF16)] * nchunks, axis=0) * bd_ref[...]

    def fold(m):
        out = m[(nchunks - 1) * ch:]
        for c in reversed(range(nchunks - 1)):
            out = jnp.where(col_chunk == c, m[c * ch:(c + 1) * ch], out)
        return out

    def side_columns(col):
        out = jnp.broadcast_to(col[(nchunks - 1) * ch:], (ch, nt))
        for c in reversed(range(nchunks - 1)):
            out = jnp.where(col_chunk == c, col[c * ch:(c + 1) * ch], out)
        return out

    def conv_silu(part, h):
        heads_per_win = DN_WIN_COLS // DN_HEAD_DIM
        win_ref = wins[3 * (h // heads_per_win) + part]
        cols = slice((h % heads_per_win) * DN_HEAD_DIM, (h % heads_per_win + 1) * DN_HEAD_DIM)
        wcols = slice(part * DN_INNER + h * DN_HEAD_DIM, part * DN_INNER + (h + 1) * DN_HEAD_DIM)
        acc = wc_ref[0:1, wcols] * win_ref[5:5 + nt, cols]
        for j in range(1, DN_CONV):
            acc = acc + wc_ref[j:j + 1, wcols] * win_ref[5 + j:5 + j + nt, cols]
        return acc * _sigmoid(acc)

    gcc, lower, attn, rhs, qd, kd = [], [], [], [], [], []
    for h in heads:
        beta = beta_all[:, h:h + 1]
        gcc.append(gc_all[:, DN_HEADS + h:DN_HEADS + h + 1])
        rem = rem_all[:, DN_HEADS + h:DN_HEADS + h + 1]
        gcr = gc_rows[DN_HEADS + h:DN_HEADS + h + 1, :]

        q = conv_silu(0, h)
        k = conv_silu(1, h)
        v = conv_silu(2, h)
        q = q * (lax.rsqrt(jnp.sum(q * q, axis=-1, keepdims=True) + L2_EPS) * (DN_HEAD_DIM ** -0.5))
        k = k * lax.rsqrt(jnp.sum(k * k, axis=-1, keepdims=True) + L2_EPS)

        decay = jnp.exp(jnp.where(causal_side, side_columns(gcc[h]) - gcr, -jnp.inf))
        egc = jnp.exp(gcc[h])
        kb = k * beta
        both = _dot(jnp.concatenate([kb, q], axis=0).astype(BF16), jnp.transpose(k).astype(BF16))
        lower.append(jnp.where(strict_side, fold(both[:nt]) * decay, 0.0))
        attn.append(to_blockdiag(fold(both[nt:]) * decay))
        rhs.append(jnp.concatenate([v * beta, kb * egc], axis=1))
        qd.append((q * egc).astype(BF16))
        kd.append((k * jnp.exp(rem)).astype(BF16))
        if h % 4 == 3:
            _take(stages, 1)

    inv = [eye_side - lower[h] for h in heads]
    power = [_dot(lower[h].astype(BF16), to_blockdiag(lower[h])) for h in heads]
    order = 2
    while order < ch:
        last = order * 2 >= ch
        for h in heads:
            wts = to_blockdiag(power[h])
            if last:
                inv[h] = inv[h] + _dot(inv[h].astype(BF16), wts)
            else:
                res = _dot(jnp.concatenate([inv[h], power[h]], axis=0).astype(BF16), wts)
                inv[h] = inv[h] + res[:ch]
                power[h] = res[ch:]
        order *= 2
    _take(stages, 1)

    u, wbf = [], []
    for h in heads:
        sol = rhs[h] + _dot(to_blockdiag(inv[h] - eye_side), rhs[h].astype(BF16))
        u.append(sol[:, :DN_HEAD_DIM])
        wbf.append(sol[:, DN_HEAD_DIM:].astype(BF16))
    _take(stages, 1)

    state = [state_ref[h] for h in heads]
    v_new = [[] for _ in heads]
    o_inter = [[] for _ in heads]
    for c in range(nchunks):
        rows = slice(c * ch, (c + 1) * ch)
        last_row = (c + 1) * ch - 1
        for h in heads:
            both = _dot(jnp.concatenate([wbf[h][rows], qd[h][rows]], axis=0), state[h].astype(BF16))
            vn = u[h][rows] - both[:ch]
            o_inter[h].append(both[ch:])
            v_new[h].append(vn)
            state[h] = (state[h] * jnp.exp(gcc[h][last_row:last_row + 1, :])
                        + _dot_tn(kd[h][rows], vn.astype(BF16)))

    for h in heads:
        state_ref[h] = state[h]
        vn = jnp.concatenate(v_new[h], axis=0).astype(BF16)
        o = jnp.concatenate(o_inter[h], axis=0) + _dot(attn[h], vn)
        o = o * lax.rsqrt(jnp.mean(o * o, axis=-1, keepdims=True) + RMS_EPS) * on_ref[...]
        zc = z_ref[:, h * DN_HEAD_DIM:(h + 1) * DN_HEAD_DIM]
        gated_ref[:, h * DN_HEAD_DIM:(h + 1) * DN_HEAD_DIM] = (o * (zc * _sigmoid(zc))).astype(BF16)

    for win_ref in wins:
        win_ref[0:8, :] = win_ref[nt:nt + 8, :]
    return x + _dot(gated_ref[...], wo_ref[...])


def _deltanet_layer_kernel(*refs, tiles, final_norm):
    x_ref = refs[0]
    mixer_refs = refs[1:8]
    lag_refs = refs[8:16]
    gf_ref, o_ref = refs[16:18]
    num_wins = 3 * DN_INNER // DN_WIN_COLS
    wins = refs[18:18 + num_wins]
    z_ref, state_ref, gated_ref, lag_ref = refs[18 + num_wins:]
    s = pl.program_id(0)
    mixer = functools.partial(_delta_mixer, x_ref[...], s % tiles == 0, mixer_refs,
                              wins, z_ref, state_ref, gated_ref)
    _run_lagged(s, tiles, lag_ref, lag_refs, gf_ref, o_ref, mixer, final_norm=final_norm,
                ff_chunk=DN_FF_CHUNK)


def _layer_call(body, h2d, mixer_args, lag_args, kv, gf, scratch, *, tile, batch, seq, final_norm, name):
    tiles = seq // tile
    steps = batch * tiles
    d = h2d.shape[1]
    mlen = kv.shape[1]
    gx, wq, wo, gm, wu, wd = lag_args

    def const(a):
        return pl.BlockSpec(a.shape, lambda s: (0,) * a.ndim)

    row = lambda v: v.reshape(1, -1)
    lag_operands = [row(gx), wq, kv, kv, wo, row(gm), wu, wd]
    lag_batch = lambda s: jnp.maximum(s - 1, 0) // tiles
    lag_specs = [
        const(lag_operands[0]), const(wq),
        pl.BlockSpec((1, mlen, d), lambda s: (lag_batch(s), 0, 0)),
        pl.BlockSpec((1, mlen, d), lambda s: (lag_batch(s), 0, 1)),
        const(wo), const(lag_operands[5]), const(wu), const(wd),
    ]
    return pl.pallas_call(
        functools.partial(body, tiles=tiles, final_norm=final_norm),
        grid=(steps + 1,),
        in_specs=[pl.BlockSpec((tile, d), lambda s: (jnp.minimum(s, steps - 1), 0))]
                 + [const(a) for a in mixer_args] + lag_specs + [const(row(gf))],
        out_specs=pl.BlockSpec((tile, d), lambda s: (jnp.maximum(s - 1, 0), 0)),
        out_shape=jax.ShapeDtypeStruct(h2d.shape, F32),
        scratch_shapes=scratch + [pltpu.VMEM((2, tile, d), F32)],
        compiler_params=_params("arbitrary"),
        name=name,
    )(h2d, *mixer_args, *lag_operands, row(gf))


def kernel(x, mem, dn_norm, dn_w_in, dn_w_conv, dn_a_log, dn_dt_bias, dn_out_norm, dn_w_out, cv_norm, cv_w_pw1, cv_b_pw1, cv_w_dw, cv_b_dw, cv_ln_g, cv_ln_b, cv_w_pw2, cv_b_pw2, xa_norm, xa_mem_norm, xa_w_q, xa_w_kv, xa_w_o, mlp_norm, mlp_w_up, mlp_w_down, final_norm):
    batch, seq, d = x.shape
    depth = xa_norm.shape[0]
    mlen = mem.shape[1]
    h = x.reshape(batch * seq, d)
    mem2d = mem.reshape(batch * mlen, d)
    row = lambda v: v.reshape(1, -1)
    chunk_of = jnp.arange(DN_TILE) // DN_CHUNK
    blockdiag_mask = (chunk_of[:, None] == chunk_of[None, :]).astype(BF16)
    for layer in range(depth):
        j = layer // 2
        kv = rms_matmul(mem2d, xa_mem_norm[layer], xa_w_kv[layer].astype(BF16), tm=KV_TILE, tn=KV_TILE,
                        out_dtype=BF16, name=f"kv{layer}").reshape(batch, mlen, 2 * d)
        lag_args = (xa_norm[layer], xa_w_q[layer].astype(BF16), xa_w_o[layer].astype(BF16),
                    mlp_norm[layer], mlp_w_up[layer].astype(BF16), mlp_w_down[layer].astype(BF16))
        common = dict(batch=batch, seq=seq, final_norm=layer == depth - 1)
        if layer % 2 == 0:
            w_in = jnp.pad(dn_w_in[j], ((0, 0), (0, DN_PROJ_COLS - dn_w_in.shape[2]))).astype(BF16)
            gate_params = jnp.zeros((8, 128), F32)
            gate_params = gate_params.at[0, DN_HEADS:2 * DN_HEADS].set(dn_a_log[j])
            gate_params = gate_params.at[1, DN_HEADS:2 * DN_HEADS].set(dn_dt_bias[j])
            mixer_args = (row(dn_norm[j]), w_in, dn_w_conv[j], gate_params, row(dn_out_norm[j]),
                          dn_w_out[j].astype(BF16), blockdiag_mask)
            scratch = [pltpu.VMEM((DN_TILE + 8, DN_WIN_COLS), F32)] * (3 * DN_INNER // DN_WIN_COLS) + [
                pltpu.VMEM((DN_TILE, DN_INNER), F32),
                pltpu.VMEM((DN_HEADS, DN_HEAD_DIM, DN_HEAD_DIM), F32),
                pltpu.VMEM((DN_TILE, DN_INNER), BF16),
            ]
            h = _layer_call(_deltanet_layer_kernel, h, mixer_args, lag_args, kv, final_norm, scratch,
                            tile=DN_TILE, name=f"deltanet_layer{layer}", **common)
        else:
            mixer_args = (row(cv_norm[j]), cv_w_pw1[j].astype(BF16), row(cv_b_pw1[j]),
                          cv_w_dw[j], row(cv_b_dw[j]), row(cv_ln_g[j]), row(cv_ln_b[j]),
                          cv_w_pw2[j].astype(BF16), row(cv_b_pw2[j]))
            scratch = [pltpu.VMEM((CV_TILE + CV_HALO, d), F32), pltpu.VMEM((CV_TILE, d), F32)]
            h = _layer_call(_conformer_layer_kernel, h, mixer_args, lag_args, kv, final_norm, scratch,
                            tile=CV_TILE, name=f"conformer_layer{layer}", **common)
    return h.reshape(batch, seq, d)
```

```python
import functools

import jax
import jax.numpy as jnp
from jax import lax
from jax.experimental import pallas as pl
from jax.experimental.pallas import tpu as pltpu

F32 = jnp.float32
BF16 = jnp.bfloat16

D_MODEL = 1024
DN_HEADS = 8
DN_HEAD_DIM = 128
DN_INNER = DN_HEADS * DN_HEAD_DIM
DN_CONV = 4
DN_CHUNK = 64
DN_PROJ_COLS = 4 * DN_INNER + 128
DN_WIN_COLS = 256
CV_WIDTH = 31
CV_HALO = 32
CV_ROWS = 512
CV_WIN_COLS = 256
CV_TIE_BACK = 1
XA_HEADS = 4
XA_HEAD_DIM = D_MODEL // XA_HEADS
DN_FF_CHUNK = 512
CV_FF_CHUNK = 512
RMS_EPS = 1e-6
LN_EPS = 1e-5
L2_EPS = 1e-6

DN_TILE = 256
CV_TILE = 512
KV_TILE = 1024
VMEM_LIMIT_BYTES = 60 * 1024 * 1024


def _params(*semantics):
    return pltpu.CompilerParams(dimension_semantics=semantics, vmem_limit_bytes=VMEM_LIMIT_BYTES)


def _sigmoid(x):
    return 0.5 * jnp.tanh(0.5 * x) + 0.5


def _rms_norm(x, g):
    ms = jnp.mean(x * x, axis=-1, keepdims=True)
    return x * lax.rsqrt(ms + RMS_EPS) * g


def _dot(a, b):
    return jnp.dot(a, b, preferred_element_type=F32)


def _dot_nt(a, b):
    return lax.dot_general(a, b, (((1,), (1,)), ((), ())), preferred_element_type=F32)


def _dot_tn(a, b):
    return lax.dot_general(a, b, (((0,), (0,)), ((), ())), preferred_element_type=F32)


def _rms_matmul_kernel(x_ref, g_ref, w_ref, o_ref, xn_ref):
    @pl.when(pl.program_id(1) == 0)
    def _():
        xn_ref[...] = _rms_norm(x_ref[...], g_ref[...]).astype(BF16)

    o_ref[...] = _dot(xn_ref[...], w_ref[...]).astype(o_ref.dtype)


def rms_matmul(x, g, w, *, tm, tn, out_dtype, name):
    m, k = x.shape
    n = w.shape[1]
    return pl.pallas_call(
        _rms_matmul_kernel,
        grid=(m // tm, n // tn),
        in_specs=[
            pl.BlockSpec((tm, k), lambda i, j: (i, 0)),
            pl.BlockSpec((1, k), lambda i, j: (0, 0)),
            pl.BlockSpec((k, tn), lambda i, j: (0, j)),
        ],
        out_specs=pl.BlockSpec((tm, tn), lambda i, j: (i, j)),
        out_shape=jax.ShapeDtypeStruct((m, n), out_dtype),
        scratch_shapes=[pltpu.VMEM((tm, k), BF16)],
        compiler_params=_params("parallel", "arbitrary"),
        name=name,
    )(x, g.reshape(1, k), w)


class _AttnMlp:
    def __init__(self, h, ff_chunk, gx_ref, wq_ref, k_ref, v_ref, wo_ref, gm_ref, wu_ref, wd_ref):
        self.h = h
        self.refs = (gx_ref, wq_ref, k_ref, v_ref, wo_ref, gm_ref, wu_ref, wd_ref)
        self.ff_chunk = ff_chunk
        self.num_chunks = wu_ref.shape[1] // ff_chunk

    def attention(self):
        gx_ref, wq_ref, k_ref, v_ref, wo_ref, gm_ref, _, _ = self.refs
        xn = _rms_norm(self.h, gx_ref[...]).astype(BF16)
        q = (_dot(xn, wq_ref[...]) * (XA_HEAD_DIM ** -0.5)).astype(BF16)
        outs = []
        for hd in range(XA_HEADS):
            cols = slice(hd * XA_HEAD_DIM, (hd + 1) * XA_HEAD_DIM)
            s = _dot_nt(q[:, cols], k_ref[0, :, cols])
            p = jnp.exp(s - jnp.max(s, axis=-1, keepdims=True))
            denom = jnp.sum(p, axis=-1, keepdims=True)
            outs.append((_dot(p.astype(BF16), v_ref[0, :, cols]) / denom).astype(BF16))
        self.acc = self.h + _dot(jnp.concatenate(outs, axis=-1), wo_ref[...])
        self.xm = _rms_norm(self.acc, gm_ref[...]).astype(BF16)
        self.hid = self._up(0)
        return self.acc[0:1, 0:128]

    def _up(self, i):
        wu_ref = self.refs[6]
        hid = _dot(self.xm, wu_ref[:, i * self.ff_chunk:(i + 1) * self.ff_chunk])
        return jnp.square(jnp.maximum(hid, 0.0)).astype(BF16)

    def mlp_chunk(self, i):
        wd_ref = self.refs[7]
        hid = self.hid
        if i + 1 < self.num_chunks:
            self.hid = self._up(i + 1)
        self.acc = self.acc + _dot(hid, wd_ref[i * self.ff_chunk:(i + 1) * self.ff_chunk, :])
        return self.acc[0:1, 0:128]


def _run_lagged(s, tiles, lag_ref, lag_refs, gf_ref, o_ref, mixer, *, final_norm, ff_chunk):
    @pl.when(s == 0)
    def _():
        lag_ref[1] = jnp.zeros(lag_ref.shape[1:], F32)

    stream = _AttnMlp(lag_ref[(s + 1) % 2], ff_chunk, *lag_refs)
    stages = [stream.attention] + [functools.partial(stream.mlp_chunk, i) for i in range(stream.num_chunks)]
    lag_ref[s % 2] = mixer(stages)
    for stage in stages:
        stage()
    out = stream.acc
    o_ref[...] = _rms_norm(out, gf_ref[...]) if final_norm else out


def _take(stages, n):
    token = None
    for _ in range(min(n, len(stages))):
        token = stages.pop(0)()
    if token is None:
        return jnp.zeros((1, 128), F32)
    return _zero_row(token)


def _zero_row(token):
    bits = lax.bitcast_convert_type(token, jnp.uint32)
    bits = lax.shift_right_logical(lax.shift_right_logical(bits, jnp.uint32(16)), jnp.uint32(16))
    return lax.bitcast_convert_type(bits, F32)


def _conformer_mixer(x, first, refs, wins, c_ref, stages):
    g_ref, w1_ref, b1_ref, wdw_ref, bdw_ref, lg_ref, lb_ref, w2_ref, b2_ref = refs
    ts = x.shape[0]

    @pl.when(first)
    def _():
        for win_ref in wins:
            win_ref[0:CV_HALO, :] = jnp.zeros((CV_HALO, CV_WIN_COLS), F32)

    xn = _rms_norm(x, g_ref[...]).astype(BF16)
    for i, win_ref in enumerate(wins):
        val = slice(i * CV_WIN_COLS, (i + 1) * CV_WIN_COLS)
        gat = slice(D_MODEL + i * CV_WIN_COLS, D_MODEL + (i + 1) * CV_WIN_COLS)
        gate = _dot(xn, w1_ref[:, gat]) + b1_ref[:, gat]
        win_ref[CV_HALO:, :] = (_dot(xn, w1_ref[:, val]) + b1_ref[:, val]) * _sigmoid(gate)
    base = CV_HALO - (CV_WIDTH - 1)
    done = []
    for cs in range(D_MODEL // 128):
        cols = slice(cs * 128, (cs + 1) * 128)
        taps = wdw_ref[:, cols] if cs < CV_TIE_BACK else wdw_ref[:, cols] + done[cs - CV_TIE_BACK]
        win_ref = wins[cs * 128 // CV_WIN_COLS]
        wcols = slice(cs * 128 % CV_WIN_COLS, cs * 128 % CV_WIN_COLS + 128)
        for r0 in range(0, ts, CV_ROWS):
            partial = {}
            for j in range(CV_WIDTH):
                start = r0 + base + j
                term = taps[j:j + 1] * win_ref[start:start + CV_ROWS, wcols]
                phase = start % 8
                partial[phase] = partial[phase] + term if phase in partial else term
            acc = partial[0] + bdw_ref[0:1, cols]
            for phase in range(1, 8):
                acc = acc + partial[phase]
            c_ref[r0:r0 + CV_ROWS, cols] = acc
        done.append(_take(stages, 1))

    for win_ref in wins:
        win_ref[0:CV_HALO, :] = win_ref[ts:ts + CV_HALO, :]

    c = c_ref[...]
    mu = jnp.mean(c, axis=-1, keepdims=True)
    xc = c - mu
    gain = lg_ref[...] + jnp.tile(done[-CV_TIE_BACK], (1, D_MODEL // 128))
    y = xc * lax.rsqrt(jnp.mean(xc * xc, axis=-1, keepdims=True) + LN_EPS) * gain + lb_ref[...]
    y = (y * _sigmoid(y)).astype(BF16)
    return x + _dot(y, w2_ref[...]) + b2_ref[...]


def _conformer_layer_kernel(*refs, tiles, final_norm):
    x_ref = refs[0]
    mixer_refs = refs[1:10]
    lag_refs = refs[10:18]
    gf_ref, o_ref = refs[18:20]
    num_wins = D_MODEL // CV_WIN_COLS
    wins = refs[20:20 + num_wins]
    c_ref, lag_ref = refs[20 + num_wins:]
    s = pl.program_id(0)
    mixer = functools.partial(_conformer_mixer, x_ref[...], s % tiles == 0, mixer_refs, wins, c_ref)
    _run_lagged(s, tiles, lag_ref, lag_refs, gf_ref, o_ref, mixer, final_norm=final_norm,
                ff_chunk=CV_FF_CHUNK)


def _delta_mixer(x, first, refs, wins, z_ref, state_ref, gated_ref, stages):
    g_ref, wi_ref, wc_ref, gp_ref, on_ref, wo_ref, bd_ref = refs
    nt = x.shape[0]
    ch = DN_CHUNK
    nchunks = nt // ch
    heads = range(DN_HEADS)

    @pl.when(first)
    def _():
        for win_ref in wins:
            win_ref[0:8, :] = jnp.zeros((8, DN_WIN_COLS), F32)
        state_ref[...] = jnp.zeros(state_ref.shape, F32)

    heads_per_win = DN_WIN_COLS // DN_HEAD_DIM

    def conv_silu(part, h):
        win_ref = wins[3 * (h // heads_per_win) + part]
        cols = slice((h % heads_per_win) * DN_HEAD_DIM, (h % heads_per_win + 1) * DN_HEAD_DIM)
        wcols = slice(part * DN_INNER + h * DN_HEAD_DIM, part * DN_INNER + (h + 1) * DN_HEAD_DIM)
        acc = wc_ref[0:1, wcols] * win_ref[5:5 + nt, cols]
        for j in range(1, DN_CONV):
            acc = acc + wc_ref[j:j + 1, wcols] * win_ref[5 + j:5 + j + nt, cols]
        return acc * _sigmoid(acc)

    qkv = {}

    def conv_norm(pair):
        for h in range(pair * heads_per_win, (pair + 1) * heads_per_win):
            q = conv_silu(0, h)
            k = conv_silu(1, h)
            v = conv_silu(2, h)
            q = q * (lax.rsqrt(jnp.sum(q * q, axis=-1, keepdims=True) + L2_EPS) * (DN_HEAD_DIM ** -0.5))
            k = k * lax.rsqrt(jnp.sum(k * k, axis=-1, keepdims=True) + L2_EPS)
            qkv[h] = (q, k, v)

    xn = _rms_norm(x, g_ref[...]).astype(BF16)
    ba = _dot(xn, wi_ref[:, 4 * DN_INNER:])
    num_pairs = DN_INNER // DN_WIN_COLS
    for pair in range(num_pairs):
        for part in range(3):
            col0 = part * DN_INNER + pair * DN_WIN_COLS
            wins[3 * pair + part][8:8 + nt, :] = _dot(xn, wi_ref[:, col0:col0 + DN_WIN_COLS])
        if pair >= 1:
            conv_norm(pair - 1)
    z_ref[...] = _dot(xn, wi_ref[:, 3 * DN_INNER:4 * DN_INNER])
    conv_norm(num_pairs - 1)

    beta_all = _sigmoid(ba)
    ap = ba + gp_ref[1:2, :]
    softplus = jnp.maximum(ap, 0.0) + jnp.log(1.0 + jnp.exp(-jnp.abs(ap)))
    g_all = -jnp.exp(gp_ref[0:1, :]) * softplus
    pos = lax.broadcasted_iota(jnp.int32, (nt, 128), 0) % ch
    gc_all = g_all
    rs_all = g_all
    shift = 1
    while shift < ch:
        gc_all = gc_all + jnp.where(pos >= shift, pltpu.roll(gc_all, shift, axis=0), 0.0)
        rs_all = rs_all + jnp.where(pos < ch - shift, pltpu.roll(rs_all, nt - shift, axis=0), 0.0)
        shift *= 2
    rem_all = rs_all - g_all
    gc_rows = jnp.transpose(gc_all)

    side_row = lax.broadcasted_iota(jnp.int32, (ch, nt), 0)
    side_col = lax.broadcasted_iota(jnp.int32, (ch, nt), 1)
    col_chunk = side_col // ch
    causal_side = side_row >= side_col % ch
    strict_side = side_row > side_col % ch
    eye_side = (side_row == side_col % ch).astype(F32)

    def to_blockdiag(x_side):
        return jnp.concatenate([x_side.astype(BF16)] * nchunks, axis=0) * bd_ref[...]

    def fold(m):
        out = m[(nchunks - 1) * ch:]
        for c in reversed(range(nchunks - 1)):
            out = jnp.where(col_chunk == c, m[c * ch:(c + 1) * ch], out)
        return out

    def side_columns(col):
        out = jnp.broadcast_to(col[(nchunks - 1) * ch:], (ch, nt))
        for c in reversed(range(nchunks - 1)):
            out = jnp.where(col_chunk == c, col[c * ch:(c + 1) * ch], out)
        return out

    gcc, lower, attn, rhs, qd, kd = [], [], [], [], [], []
    for h in heads:
        beta = beta_all[:, h:h + 1]
        gcc.append(gc_all[:, DN_HEADS + h:DN_HEADS + h + 1])
        rem = rem_all[:, DN_HEADS + h:DN_HEADS + h + 1]
        gcr = gc_rows[DN_HEADS + h:DN_HEADS + h + 1, :]

        q, k, v = qkv[h]
        decay = jnp.exp(jnp.where(causal_side, side_columns(gcc[h]) - gcr, -jnp.inf))
        egc = jnp.exp(gcc[h])
        kb = k * beta
        both = _dot(jnp.concatenate([kb, q], axis=0).astype(BF16), jnp.transpose(k).astype(BF16))
        lower.append(jnp.where(strict_side, fold(both[:nt]) * decay, 0.0))
        attn.append(to_blockdiag(fold(both[nt:]) * decay))
        rhs.append(jnp.concatenate([v * beta, kb * egc], axis=1))
        qd.append((q * egc).astype(BF16))
        kd.append((k * jnp.exp(rem)).astype(BF16))
        _take(stages, 1)

    inv = [eye_side - lower[h] for h in heads]
    power = [_dot(lower[h].astype(BF16), to_blockdiag(lower[h])) for h in heads]
    order = 2
    while order < ch:
        last = order * 2 >= ch
        for h in heads:
            wts = to_blockdiag(power[h])
            if last:
                inv[h] = inv[h] + _dot(inv[h].astype(BF16), wts)
            else:
                res = _dot(jnp.concatenate([inv[h], power[h]], axis=0).astype(BF16), wts)
                inv[h] = inv[h] + res[:ch]
                power[h] = res[ch:]
        order *= 2
    _take(stages, 1)

    u, wbf = [], []
    for h in heads:
        sol = rhs[h] + _dot(to_blockdiag(inv[h] - eye_side), rhs[h].astype(BF16))
        u.append(sol[:, :DN_HEAD_DIM])
        wbf.append(sol[:, DN_HEAD_DIM:].astype(BF16))
    _take(stages, 1)

    state = [state_ref[h] for h in heads]
    v_new = [[] for _ in heads]
    o_inter = [[] for _ in heads]
    for c in range(nchunks):
        rows = slice(c * ch, (c + 1) * ch)
        last_row = (c + 1) * ch - 1
        for h in heads:
            both = _dot(jnp.concatenate([wbf[h][rows], qd[h][rows]], axis=0), state[h].astype(BF16))
            vn = u[h][rows] - both[:ch]
            o_inter[h].append(both[ch:])
            v_new[h].append(vn)
            state[h] = (state[h] * jnp.exp(gcc[h][last_row:last_row + 1, :])
                        + _dot_tn(kd[h][rows], vn.astype(BF16)))

    for h in heads:
        state_ref[h] = state[h]
        vn = jnp.concatenate(v_new[h], axis=0).astype(BF16)
        o = jnp.concatenate(o_inter[h], axis=0) + _dot(attn[h], vn)
        o = o * lax.rsqrt(jnp.mean(o * o, axis=-1, keepdims=True) + RMS_EPS) * on_ref[...]
        zc = z_ref[:, h * DN_HEAD_DIM:(h + 1) * DN_HEAD_DIM]
        gated_ref[:, h * DN_HEAD_DIM:(h + 1) * DN_HEAD_DIM] = (o * (zc * _sigmoid(zc))).astype(BF16)

    for win_ref in wins:
        win_ref[0:8, :] = win_ref[nt:nt + 8, :]
    return x + _dot(gated_ref[...], wo_ref[...])


def _deltanet_layer_kernel(*refs, tiles, final_norm):
    x_ref = refs[0]
    mixer_refs = refs[1:8]
    lag_refs = refs[8:16]
    gf_ref, o_ref = refs[16:18]
    num_wins = 3 * DN_INNER // DN_WIN_COLS
    wins = refs[18:18 + num_wins]
    z_ref, state_ref, gated_ref, lag_ref = refs[18 + num_wins:]
    s = pl.program_id(0)
    mixer = functools.partial(_delta_mixer, x_ref[...], s % tiles == 0, mixer_refs,
                              wins, z_ref, state_ref, gated_ref)
    _run_lagged(s, tiles, lag_ref, lag_refs, gf_ref, o_ref, mixer, final_norm=final_norm,
                ff_chunk=DN_FF_CHUNK)


def _layer_call(body, h2d, mixer_args, lag_args, kv, gf, scratch, *, tile, batch, seq, final_norm, name):
    tiles = seq // tile
    steps = batch * tiles
    d = h2d.shape[1]
    mlen = kv.shape[1]
    gx, wq, wo, gm, wu, wd = lag_args

    def const(a):
        return pl.BlockSpec(a.shape, lambda s: (0,) * a.ndim)

    row = lambda v: v.reshape(1, -1)
    lag_operands = [row(gx), wq, kv, kv, wo, row(gm), wu, wd]
    lag_batch = lambda s: jnp.maximum(s - 1, 0) // tiles
    lag_specs = [
        const(lag_operands[0]), const(wq),
        pl.BlockSpec((1, mlen, d), lambda s: (lag_batch(s), 0, 0)),
        pl.BlockSpec((1, mlen, d), lambda s: (lag_batch(s), 0, 1)),
        const(wo), const(lag_operands[5]), const(wu), const(wd),
    ]
    return pl.pallas_call(
        functools.partial(body, tiles=tiles, final_norm=final_norm),
        grid=(steps + 1,),
        in_specs=[pl.BlockSpec((tile, d), lambda s: (jnp.minimum(s, steps - 1), 0))]
                 + [const(a) for a in mixer_args] + lag_specs + [const(row(gf))],
        out_specs=pl.BlockSpec((tile, d), lambda s: (jnp.maximum(s - 1, 0), 0)),
        out_shape=jax.ShapeDtypeStruct(h2d.shape, F32),
        scratch_shapes=scratch + [pltpu.VMEM((2, tile, d), F32)],
        compiler_params=_params("arbitrary"),
        name=name,
    )(h2d, *mixer_args, *lag_operands, row(gf))


def kernel(x, mem, dn_norm, dn_w_in, dn_w_conv, dn_a_log, dn_dt_bias, dn_out_norm, dn_w_out, cv_norm, cv_w_pw1, cv_b_pw1, cv_w_dw, cv_b_dw, cv_ln_g, cv_ln_b, cv_w_pw2, cv_b_pw2, xa_norm, xa_mem_norm, xa_w_q, xa_w_kv, xa_w_o, mlp_norm, mlp_w_up, mlp_w_down, final_norm):
    batch, seq, d = x.shape
    depth = xa_norm.shape[0]
    mlen = mem.shape[1]
    h = x.reshape(batch * seq, d)
    mem2d = mem.reshape(batch * mlen, d)
    row = lambda v: v.reshape(1, -1)
    chunk_of = jnp.arange(DN_TILE) // DN_CHUNK
    blockdiag_mask = (chunk_of[:, None] == chunk_of[None, :]).astype(BF16)
    for layer in range(depth):
        j = layer // 2
        kv = rms_matmul(mem2d, xa_mem_norm[layer], xa_w_kv[layer].astype(BF16), tm=KV_TILE, tn=KV_TILE,
                        out_dtype=BF16, name=f"kv{layer}").reshape(batch, mlen, 2 * d)
        lag_args = (xa_norm[layer], xa_w_q[layer].astype(BF16), xa_w_o[layer].astype(BF16),
                    mlp_norm[layer], mlp_w_up[layer].astype(BF16), mlp_w_down[layer].astype(BF16))
        common = dict(batch=batch, seq=seq, final_norm=layer == depth - 1)
        if layer % 2 == 0:
            w_in = jnp.pad(dn_w_in[j], ((0, 0), (0, DN_PROJ_COLS - dn_w_in.shape[2]))).astype(BF16)
            gate_params = jnp.zeros((8, 128), F32)
            gate_params = gate_params.at[0, DN_HEADS:2 * DN_HEADS].set(dn_a_log[j])
            gate_params = gate_params.at[1, DN_HEADS:2 * DN_HEADS].set(dn_dt_bias[j])
            mixer_args = (row(dn_norm[j]), w_in, dn_w_conv[j], gate_params, row(dn_out_norm[j]),
                          dn_w_out[j].astype(BF16), blockdiag_mask)
            scratch = [pltpu.VMEM((DN_TILE + 8, DN_WIN_COLS), F32)] * (3 * DN_INNER // DN_WIN_COLS) + [
                pltpu.VMEM((DN_TILE, DN_INNER), F32),
                pltpu.VMEM((DN_HEADS, DN_HEAD_DIM, DN_HEAD_DIM), F32),
                pltpu.VMEM((DN_TILE, DN_INNER), BF16),
            ]
            h = _layer_call(_deltanet_layer_kernel, h, mixer_args, lag_args, kv, final_norm, scratch,
                            tile=DN_TILE, name=f"deltanet_layer{layer}", **common)
        else:
            mixer_args = (row(cv_norm[j]), cv_w_pw1[j].astype(BF16), row(cv_b_pw1[j]),
                          cv_w_dw[j], row(cv_b_dw[j]), row(cv_ln_g[j]), row(cv_ln_b[j]),
                          cv_w_pw2[j].astype(BF16), row(cv_b_pw2[j]))
            scratch = ([pltpu.VMEM((CV_TILE + CV_HALO, CV_WIN_COLS), F32)] * (d // CV_WIN_COLS)
                       + [pltpu.VMEM((CV_TILE, d), F32)])
            h = _layer_call(_conformer_layer_kernel, h, mixer_args, lag_args, kv, final_norm, scratch,
                            tile=CV_TILE, name=f"conformer_layer{layer}", **common)
    return h.reshape(batch, seq, d)
```

```python
import functools

import jax
import jax.numpy as jnp
from jax import lax
from jax.experimental import pallas as pl
from jax.experimental.pallas import tpu as pltpu

F32 = jnp.float32
BF16 = jnp.bfloat16

D_MODEL = 1024
DN_HEADS = 8
DN_HEAD_DIM = 128
DN_INNER = DN_HEADS * DN_HEAD_DIM
DN_CONV = 4
DN_CHUNK = 64
DN_PROJ_COLS = 4 * DN_INNER + 128
DN_WIN_COLS = 256
CV_WIDTH = 31
CV_HALO = 32
CV_ROWS = 512
CV_WIN_COLS = 256
CV_TIE_BACK = 1
XA_HEADS = 4
XA_HEAD_DIM = D_MODEL // XA_HEADS
DN_FF_CHUNK = 512
CV_FF_CHUNK = 256
RMS_EPS = 1e-6
LN_EPS = 1e-5
L2_EPS = 1e-6

DN_TILE = 256
CV_TILE = 512
KV_TILE = 1024
VMEM_LIMIT_BYTES = 60 * 1024 * 1024


def _params(*semantics):
    return pltpu.CompilerParams(dimension_semantics=semantics, vmem_limit_bytes=VMEM_LIMIT_BYTES)


def _sigmoid(x):
    return 0.5 * jnp.tanh(0.5 * x) + 0.5


def _rms_norm(x, g):
    ms = jnp.mean(x * x, axis=-1, keepdims=True)
    return x * lax.rsqrt(ms + RMS_EPS) * g


def _dot(a, b):
    return jnp.dot(a, b, preferred_element_type=F32)


def _dot_nt(a, b):
    return lax.dot_general(a, b, (((1,), (1,)), ((), ())), preferred_element_type=F32)


def _dot_tn(a, b):
    return lax.dot_general(a, b, (((0,), (0,)), ((), ())), preferred_element_type=F32)


def _rms_matmul_kernel(x_ref, g_ref, w_ref, o_ref, xn_ref):
    @pl.when(pl.program_id(1) == 0)
    def _():
        xn_ref[...] = _rms_norm(x_ref[...], g_ref[...]).astype(BF16)

    o_ref[...] = _dot(xn_ref[...], w_ref[...]).astype(o_ref.dtype)


def rms_matmul(x, g, w, *, tm, tn, out_dtype, name):
    m, k = x.shape
    n = w.shape[1]
    return pl.pallas_call(
        _rms_matmul_kernel,
        grid=(m // tm, n // tn),
        in_specs=[
            pl.BlockSpec((tm, k), lambda i, j: (i, 0)),
            pl.BlockSpec((1, k), lambda i, j: (0, 0)),
            pl.BlockSpec((k, tn), lambda i, j: (0, j)),
        ],
        out_specs=pl.BlockSpec((tm, tn), lambda i, j: (i, j)),
        out_shape=jax.ShapeDtypeStruct((m, n), out_dtype),
        scratch_shapes=[pltpu.VMEM((tm, k), BF16)],
        compiler_params=_params("parallel", "arbitrary"),
        name=name,
    )(x, g.reshape(1, k), w)


class _AttnMlp:
    def __init__(self, h, ff_chunk, gx_ref, wq_ref, k_ref, v_ref, wo_ref, gm_ref, wu_ref, wd_ref):
        self.h = h
        self.refs = (gx_ref, wq_ref, k_ref, v_ref, wo_ref, gm_ref, wu_ref, wd_ref)
        self.ff_chunk = ff_chunk
        self.num_chunks = wu_ref.shape[1] // ff_chunk

    def attention(self):
        gx_ref, wq_ref, k_ref, v_ref, wo_ref, gm_ref, _, _ = self.refs
        xn = _rms_norm(self.h, gx_ref[...]).astype(BF16)
        q = (_dot(xn, wq_ref[...]) * (XA_HEAD_DIM ** -0.5)).astype(BF16)
        outs = []
        for hd in range(XA_HEADS):
            cols = slice(hd * XA_HEAD_DIM, (hd + 1) * XA_HEAD_DIM)
            s = _dot_nt(q[:, cols], k_ref[0, :, cols])
            p = jnp.exp(s - jnp.max(s, axis=-1, keepdims=True))
            denom = jnp.sum(p, axis=-1, keepdims=True)
            outs.append((_dot(p.astype(BF16), v_ref[0, :, cols]) / denom).astype(BF16))
        self.acc = self.h + _dot(jnp.concatenate(outs, axis=-1), wo_ref[...])
        self.xm = _rms_norm(self.acc, gm_ref[...]).astype(BF16)
        self.hid = self._up(0)
        return self.acc[0:1, 0:128]

    def _up(self, i):
        wu_ref = self.refs[6]
        hid = _dot(self.xm, wu_ref[:, i * self.ff_chunk:(i + 1) * self.ff_chunk])
        return jnp.square(jnp.maximum(hid, 0.0)).astype(BF16)

    def mlp_chunk(self, i):
        wd_ref = self.refs[7]
        hid = self.hid
        if i + 1 < self.num_chunks:
            self.hid = self._up(i + 1)
        self.acc = self.acc + _dot(hid, wd_ref[i * self.ff_chunk:(i + 1) * self.ff_chunk, :])
        return self.acc[0:1, 0:128]


def _run_lagged(s, tiles, lag_ref, lag_refs, gf_ref, o_ref, mixer, *, final_norm, ff_chunk):
    @pl.when(s == 0)
    def _():
        lag_ref[1] = jnp.zeros(lag_ref.shape[1:], F32)

    stream = _AttnMlp(lag_ref[(s + 1) % 2], ff_chunk, *lag_refs)
    stages = [stream.attention] + [functools.partial(stream.mlp_chunk, i) for i in range(stream.num_chunks)]
    lag_ref[s % 2] = mixer(stages)
    for stage in stages:
        stage()
    out = stream.acc
    o_ref[...] = _rms_norm(out, gf_ref[...]) if final_norm else out


def _take(stages, n):
    token = None
    for _ in range(min(n, len(stages))):
        token = stages.pop(0)()
    if token is None:
        return jnp.zeros((1, 128), F32)
    return _zero_row(token)


def _zero_row(token):
    bits = lax.bitcast_convert_type(token, jnp.uint32)
    bits = lax.shift_right_logical(lax.shift_right_logical(bits, jnp.uint32(16)), jnp.uint32(16))
    return lax.bitcast_convert_type(bits, F32)


def _conformer_mixer(x, first, refs, wins, c_ref, stages):
    g_ref, w1_ref, b1_ref, wdw_ref, bdw_ref, lg_ref, lb_ref, w2_ref, b2_ref = refs
    ts = x.shape[0]

    @pl.when(first)
    def _():
        for win_ref in wins:
            win_ref[0:CV_HALO, :] = jnp.zeros((CV_HALO, CV_WIN_COLS), F32)

    xn = _rms_norm(x, g_ref[...]).astype(BF16)
    for i, win_ref in enumerate(wins):
        val = slice(i * CV_WIN_COLS, (i + 1) * CV_WIN_COLS)
        gat = slice(D_MODEL + i * CV_WIN_COLS, D_MODEL + (i + 1) * CV_WIN_COLS)
        gate = _dot(xn, w1_ref[:, gat]) + b1_ref[:, gat]
        win_ref[CV_HALO:, :] = (_dot(xn, w1_ref[:, val]) + b1_ref[:, val]) * _sigmoid(gate)
    base = CV_HALO - (CV_WIDTH - 1)
    done = []
    for cs in range(D_MODEL // 128):
        cols = slice(cs * 128, (cs + 1) * 128)
        taps = wdw_ref[:, cols] if cs < CV_TIE_BACK else wdw_ref[:, cols] + done[cs - CV_TIE_BACK]
        win_ref = wins[cs * 128 // CV_WIN_COLS]
        wcols = slice(cs * 128 % CV_WIN_COLS, cs * 128 % CV_WIN_COLS + 128)
        for r0 in range(0, ts, CV_ROWS):
            partial = {}
            for j in range(CV_WIDTH):
                start = r0 + base + j
                term = taps[j:j + 1] * win_ref[start:start + CV_ROWS, wcols]
                phase = start % 8
                partial[phase] = partial[phase] + term if phase in partial else term
            acc = partial[0] + bdw_ref[0:1, cols]
            for phase in range(1, 8):
                acc = acc + partial[phase]
            c_ref[r0:r0 + CV_ROWS, cols] = acc
        done.append(_take(stages, 2))

    for win_ref in wins:
        win_ref[0:CV_HALO, :] = win_ref[ts:ts + CV_HALO, :]

    c = c_ref[...]
    mu = jnp.mean(c, axis=-1, keepdims=True)
    xc = c - mu
    gain = lg_ref[...] + jnp.tile(done[-CV_TIE_BACK], (1, D_MODEL // 128))
    y = xc * lax.rsqrt(jnp.mean(xc * xc, axis=-1, keepdims=True) + LN_EPS) * gain + lb_ref[...]
    y = (y * _sigmoid(y)).astype(BF16)
    return x + _dot(y, w2_ref[...]) + b2_ref[...]


def _conformer_layer_kernel(*refs, tiles, final_norm):
    x_ref = refs[0]
    mixer_refs = refs[1:10]
    lag_refs = refs[10:18]
    gf_ref, o_ref = refs[18:20]
    num_wins = D_MODEL // CV_WIN_COLS
    wins = refs[20:20 + num_wins]
    c_ref, lag_ref = refs[20 + num_wins:]
    s = pl.program_id(0)
    mixer = functools.partial(_conformer_mixer, x_ref[...], s % tiles == 0, mixer_refs, wins, c_ref)
    _run_lagged(s, tiles, lag_ref, lag_refs, gf_ref, o_ref, mixer, final_norm=final_norm,
                ff_chunk=CV_FF_CHUNK)


def _delta_mixer(x, first, refs, wins, z_ref, state_ref, gated_ref, stages):
    g_ref, wi_ref, wc_ref, gp_ref, on_ref, wo_ref, bd_ref = refs
    nt = x.shape[0]
    ch = DN_CHUNK
    nchunks = nt // ch
    heads = range(DN_HEADS)

    @pl.when(first)
    def _():
        for win_ref in wins:
            win_ref[0:8, :] = jnp.zeros((8, DN_WIN_COLS), F32)
        state_ref[...] = jnp.zeros(state_ref.shape, F32)

    heads_per_win = DN_WIN_COLS // DN_HEAD_DIM

    def conv_silu(part, h):
        win_ref = wins[3 * (h // heads_per_win) + part]
        cols = slice((h % heads_per_win) * DN_HEAD_DIM, (h % heads_per_win + 1) * DN_HEAD_DIM)
        wcols = slice(part * DN_INNER + h * DN_HEAD_DIM, part * DN_INNER + (h + 1) * DN_HEAD_DIM)
        acc = wc_ref[0:1, wcols] * win_ref[5:5 + nt, cols]
        for j in range(1, DN_CONV):
            acc = acc + wc_ref[j:j + 1, wcols] * win_ref[5 + j:5 + j + nt, cols]
        return acc * _sigmoid(acc)

    qkv = {}

    def conv_norm(pair):
        for h in range(pair * heads_per_win, (pair + 1) * heads_per_win):
            q = conv_silu(0, h)
            k = conv_silu(1, h)
            v = conv_silu(2, h)
            q = q * (lax.rsqrt(jnp.sum(q * q, axis=-1, keepdims=True) + L2_EPS) * (DN_HEAD_DIM ** -0.5))
            k = k * lax.rsqrt(jnp.sum(k * k, axis=-1, keepdims=True) + L2_EPS)
            qkv[h] = (q, k, v)

    xn = _rms_norm(x, g_ref[...]).astype(BF16)
    ba = _dot(xn, wi_ref[:, 4 * DN_INNER:])
    num_pairs = DN_INNER // DN_WIN_COLS
    for pair in range(num_pairs):
        for part in range(3):
            col0 = part * DN_INNER + pair * DN_WIN_COLS
            wins[3 * pair + part][8:8 + nt, :] = _dot(xn, wi_ref[:, col0:col0 + DN_WIN_COLS])
        if pair >= 1:
            conv_norm(pair - 1)
    z_ref[...] = _dot(xn, wi_ref[:, 3 * DN_INNER:4 * DN_INNER])
    conv_norm(num_pairs - 1)

    beta_all = _sigmoid(ba)
    ap = ba + gp_ref[1:2, :]
    softplus = jnp.maximum(ap, 0.0) + jnp.log(1.0 + jnp.exp(-jnp.abs(ap)))
    g_all = -jnp.exp(gp_ref[0:1, :]) * softplus
    pos = lax.broadcasted_iota(jnp.int32, (nt, 128), 0) % ch
    gc_all = g_all
    rs_all = g_all
    shift = 1
    while shift < ch:
        gc_all = gc_all + jnp.where(pos >= shift, pltpu.roll(gc_all, shift, axis=0), 0.0)
        rs_all = rs_all + jnp.where(pos < ch - shift, pltpu.roll(rs_all, nt - shift, axis=0), 0.0)
        shift *= 2
    rem_all = rs_all - g_all
    gc_rows = jnp.transpose(gc_all)

    side_row = lax.broadcasted_iota(jnp.int32, (ch, nt), 0)
    side_col = lax.broadcasted_iota(jnp.int32, (ch, nt), 1)
    col_chunk = side_col // ch
    causal_side = side_row >= side_col % ch
    strict_side = side_row > side_col % ch
    eye_side = (side_row == side_col % ch).astype(F32)

    def to_blockdiag(x_side):
        return jnp.concatenate([x_side.astype(BF16)] * nchunks, axis=0) * bd_ref[...]

    def fold(m):
        out = m[(nchunks - 1) * ch:]
        for c in reversed(range(nchunks - 1)):
            out = jnp.where(col_chunk == c, m[c * ch:(c + 1) * ch], out)
        return out

    def side_columns(col):
        out = jnp.broadcast_to(col[(nchunks - 1) * ch:], (ch, nt))
        for c in reversed(range(nchunks - 1)):
            out = jnp.where(col_chunk == c, col[c * ch:(c + 1) * ch], out)
        return out

    gcc, lower, attn, rhs, qd, kd = [], [], [], [], [], []
    for h in heads:
        beta = beta_all[:, h:h + 1]
        gcc.append(gc_all[:, DN_HEADS + h:DN_HEADS + h + 1])
        rem = rem_all[:, DN_HEADS + h:DN_HEADS + h + 1]
        gcr = gc_rows[DN_HEADS + h:DN_HEADS + h + 1, :]

        q, k, v = qkv[h]
        decay = jnp.exp(jnp.where(causal_side, side_columns(gcc[h]) - gcr, -jnp.inf))
        egc = jnp.exp(gcc[h])
        kb = k * beta
        both = _dot(jnp.concatenate([kb, q], axis=0).astype(BF16), jnp.transpose(k).astype(BF16))
        lower.append(jnp.where(strict_side, fold(both[:nt]) * decay, 0.0))
        attn.append(to_blockdiag(fold(both[nt:]) * decay))
        rhs.append(jnp.concatenate([v * beta, kb * egc], axis=1))
        qd.append((q * egc).astype(BF16))
        kd.append((k * jnp.exp(rem)).astype(BF16))
        _take(stages, 1)

    inv = [eye_side - lower[h] for h in heads]
    power = [_dot(lower[h].astype(BF16), to_blockdiag(lower[h])) for h in heads]
    order = 2
    while order < ch:
        last = order * 2 >= ch
        for h in heads:
            wts = to_blockdiag(power[h])
            if last:
                inv[h] = inv[h] + _dot(inv[h].astype(BF16), wts)
            else:
                res = _dot(jnp.concatenate([inv[h], power[h]], axis=0).astype(BF16), wts)
                inv[h] = inv[h] + res[:ch]
                power[h] = res[ch:]
        order *= 2
    _take(stages, 1)

    u, wbf = [], []
    for h in heads:
        sol = rhs[h] + _dot(to_blockdiag(inv[h] - eye_side), rhs[h].astype(BF16))
        u.append(sol[:, :DN_HEAD_DIM])
        wbf.append(sol[:, DN_HEAD_DIM:].astype(BF16))
    _take(stages, 1)

    state = [state_ref[h] for h in heads]
    v_new = [[] for _ in heads]
    o_inter = [[] for _ in heads]
    for c in range(nchunks):
        rows = slice(c * ch, (c + 1) * ch)
        last_row = (c + 1) * ch - 1
        for h in heads:
            both = _dot(jnp.concatenate([wbf[h][rows], qd[h][rows]], axis=0), state[h].astype(BF16))
            vn = u[h][rows] - both[:ch]
            o_inter[h].append(both[ch:])
            v_new[h].append(vn)
            state[h] = (state[h] * jnp.exp(gcc[h][last_row:last_row + 1, :])
                        + _dot_tn(kd[h][rows], vn.astype(BF16)))

    for h in heads:
        state_ref[h] = state[h]
        vn = jnp.concatenate(v_new[h], axis=0).astype(BF16)
        o = jnp.concatenate(o_inter[h], axis=0) + _dot(attn[h], vn)
        o = o * lax.rsqrt(jnp.mean(o * o, axis=-1, keepdims=True) + RMS_EPS) * on_ref[...]
        zc = z_ref[:, h * DN_HEAD_DIM:(h + 1) * DN_HEAD_DIM]
        gated_ref[:, h * DN_HEAD_DIM:(h + 1) * DN_HEAD_DIM] = (o * (zc * _sigmoid(zc))).astype(BF16)

    for win_ref in wins:
        win_ref[0:8, :] = win_ref[nt:nt + 8, :]
    return x + _dot(gated_ref[...], wo_ref[...])


def _deltanet_layer_kernel(*refs, tiles, final_norm):
    x_ref = refs[0]
    mixer_refs = refs[1:8]
    lag_refs = refs[8:16]
    gf_ref, o_ref = refs[16:18]
    num_wins = 3 * DN_INNER // DN_WIN_COLS
    wins = refs[18:18 + num_wins]
    z_ref, state_ref, gated_ref, lag_ref = refs[18 + num_wins:]
    s = pl.program_id(0)
    mixer = functools.partial(_delta_mixer, x_ref[...], s % tiles == 0, mixer_refs,
                              wins, z_ref, state_ref, gated_ref)
    _run_lagged(s, tiles, lag_ref, lag_refs, gf_ref, o_ref, mixer, final_norm=final_norm,
                ff_chunk=DN_FF_CHUNK)


def _layer_call(body, h2d, mixer_args, lag_args, kv, gf, scratch, *, tile, batch, seq, final_norm, name):
    tiles = seq // tile
    steps = batch * tiles
    d = h2d.shape[1]
    mlen = kv.shape[1]
    gx, wq, wo, gm, wu, wd = lag_args

    def const(a):
        return pl.BlockSpec(a.shape, lambda s: (0,) * a.ndim)

    row = lambda v: v.reshape(1, -1)
    lag_operands = [row(gx), wq, kv, kv, wo, row(gm), wu, wd]
    lag_batch = lambda s: jnp.maximum(s - 1, 0) // tiles
    lag_specs = [
        const(lag_operands[0]), const(wq),
        pl.BlockSpec((1, mlen, d), lambda s: (lag_batch(s), 0, 0)),
        pl.BlockSpec((1, mlen, d), lambda s: (lag_batch(s), 0, 1)),
        const(wo), const(lag_operands[5]), const(wu), const(wd),
    ]
    return pl.pallas_call(
        functools.partial(body, tiles=tiles, final_norm=final_norm),
        grid=(steps + 1,),
        in_specs=[pl.BlockSpec((tile, d), lambda s: (jnp.minimum(s, steps - 1), 0))]
                 + [const(a) for a in mixer_args] + lag_specs + [const(row(gf))],
        out_specs=pl.BlockSpec((tile, d), lambda s: (jnp.maximum(s - 1, 0), 0)),
        out_shape=jax.ShapeDtypeStruct(h2d.shape, F32),
        scratch_shapes=scratch + [pltpu.VMEM((2, tile, d), F32)],
        compiler_params=_params("arbitrary"),
        name=name,
    )(h2d, *mixer_args, *lag_operands, row(gf))


def kernel(x, mem, dn_norm, dn_w_in, dn_w_conv, dn_a_log, dn_dt_bias, dn_out_norm, dn_w_out, cv_norm, cv_w_pw1, cv_b_pw1, cv_w_dw, cv_b_dw, cv_ln_g, cv_ln_b, cv_w_pw2, cv_b_pw2, xa_norm, xa_mem_norm, xa_w_q, xa_w_kv, xa_w_o, mlp_norm, mlp_w_up, mlp_w_down, final_norm):
    batch, seq, d = x.shape
    depth = xa_norm.shape[0]
    mlen = mem.shape[1]
    h = x.reshape(batch * seq, d)
    mem2d = mem.reshape(batch * mlen, d)
    row = lambda v: v.reshape(1, -1)
    chunk_of = jnp.arange(DN_TILE) // DN_CHUNK
    blockdiag_mask = (chunk_of[:, None] == chunk_of[None, :]).astype(BF16)
    for layer in range(depth):
        j = layer // 2
        kv = rms_matmul(mem2d, xa_mem_norm[layer], xa_w_kv[layer].astype(BF16), tm=KV_TILE, tn=KV_TILE,
                        out_dtype=BF16, name=f"kv{layer}").reshape(batch, mlen, 2 * d)
        lag_args = (xa_norm[layer], xa_w_q[layer].astype(BF16), xa_w_o[layer].astype(BF16),
                    mlp_norm[layer], mlp_w_up[layer].astype(BF16), mlp_w_down[layer].astype(BF16))
        common = dict(batch=batch, seq=seq, final_norm=layer == depth - 1)
        if layer % 2 == 0:
            w_in = jnp.pad(dn_w_in[j], ((0, 0), (0, DN_PROJ_COLS - dn_w_in.shape[2]))).astype(BF16)
            gate_params = jnp.zeros((8, 128), F32)
            gate_params = gate_params.at[0, DN_HEADS:2 * DN_HEADS].set(dn_a_log[j])
            gate_params = gate_params.at[1, DN_HEADS:2 * DN_HEADS].set(dn_dt_bias[j])
            mixer_args = (row(dn_norm[j]), w_in, dn_w_conv[j], gate_params, row(dn_out_norm[j]),
                          dn_w_out[j].astype(BF16), blockdiag_mask)
            scratch = [pltpu.VMEM((DN_TILE + 8, DN_WIN_COLS), F32)] * (3 * DN_INNER // DN_WIN_COLS) + [
                pltpu.VMEM((DN_TILE, DN_INNER), F32),
                pltpu.VMEM((DN_HEADS, DN_HEAD_DIM, DN_HEAD_DIM), F32),
                pltpu.VMEM((DN_TILE, DN_INNER), BF16),
            ]
            h = _layer_call(_deltanet_layer_kernel, h, mixer_args, lag_args, kv, final_norm, scratch,
                            tile=DN_TILE, name=f"deltanet_layer{layer}", **common)
        else:
            mixer_args = (row(cv_norm[j]), cv_w_pw1[j].astype(BF16), row(cv_b_pw1[j]),
                          cv_w_dw[j], row(cv_b_dw[j]), row(cv_ln_g[j]), row(cv_ln_b[j]),
                          cv_w_pw2[j].astype(BF16), row(cv_b_pw2[j]))
            scratch = ([pltpu.VMEM((CV_TILE + CV_HALO, CV_WIN_COLS), F32)] * (d // CV_WIN_COLS)
                       + [pltpu.VMEM((CV_TILE, d), F32)])
            h = _layer_call(_conformer_layer_kernel, h, mixer_args, lag_args, kv, final_norm, scratch,
                            tile=CV_TILE, name=f"conformer_layer{layer}", **common)
    return h.reshape(batch, seq, d)
```

```python
import functools

import jax
import jax.numpy as jnp
from jax import lax
from jax.experimental import pallas as pl
from jax.experimental.pallas import tpu as pltpu

F32 = jnp.float32
BF16 = jnp.bfloat16

D_MODEL = 1024
DN_HEADS = 8
DN_HEAD_DIM = 128
DN_INNER = DN_HEADS * DN_HEAD_DIM
DN_CONV = 4
DN_CHUNK = 64
DN_PROJ_COLS = 4 * DN_INNER + 128
DN_WIN_COLS = 256
CV_WIDTH = 31
CV_HALO = 32
CV_ROWS = 512
CV_WIN_COLS = 256
CV_TIE_BACK = 1
XA_HEADS = 4
XA_HEAD_DIM = D_MODEL // XA_HEADS
DN_FF_CHUNK = 512
CV_FF_CHUNK = 256
RMS_EPS = 1e-6
LN_EPS = 1e-5
L2_EPS = 1e-6

DN_TILE = 256
CV_TILE = 512
KV_TILE = 1024
VMEM_LIMIT_BYTES = 60 * 1024 * 1024


def _params(*semantics):
    return pltpu.CompilerParams(dimension_semantics=semantics, vmem_limit_bytes=VMEM_LIMIT_BYTES)


def _sigmoid(x):
    return 0.5 * jnp.tanh(0.5 * x) + 0.5


def _rms_norm(x, g):
    ms = jnp.mean(x * x, axis=-1, keepdims=True)
    return x * lax.rsqrt(ms + RMS_EPS) * g


def _dot(a, b):
    return jnp.dot(a, b, preferred_element_type=F32)


def _dot_nt(a, b):
    return lax.dot_general(a, b, (((1,), (1,)), ((), ())), preferred_element_type=F32)


def _dot_tn(a, b):
    return lax.dot_general(a, b, (((0,), (0,)), ((), ())), preferred_element_type=F32)


def _rms_matmul_kernel(x_ref, g_ref, w_ref, o_ref, xn_ref):
    @pl.when(pl.program_id(2) == 0)
    def _():
        xn_ref[...] = _rms_norm(x_ref[...], g_ref[0]).astype(BF16)

    o_ref[0] = _dot(xn_ref[...], w_ref[0]).astype(o_ref.dtype)


def rms_matmul_layers(x, g, w, *, tm, tn, out_dtype, name):
    m, k = x.shape
    layers, _, n = w.shape
    return pl.pallas_call(
        _rms_matmul_kernel,
        grid=(layers, m // tm, n // tn),
        in_specs=[
            pl.BlockSpec((tm, k), lambda l, i, j: (i, 0)),
            pl.BlockSpec((1, 1, k), lambda l, i, j: (l, 0, 0)),
            pl.BlockSpec((1, k, tn), lambda l, i, j: (l, 0, j)),
        ],
        out_specs=pl.BlockSpec((1, tm, tn), lambda l, i, j: (l, i, j)),
        out_shape=jax.ShapeDtypeStruct((layers, m, n), out_dtype),
        scratch_shapes=[pltpu.VMEM((tm, k), BF16)],
        compiler_params=_params("parallel", "parallel", "arbitrary"),
        name=name,
    )(x, g.reshape(layers, 1, k), w)


class _AttnMlp:
    def __init__(self, h, ff_chunk, gx_ref, wq_ref, k_ref, v_ref, wo_ref, gm_ref, wu_ref, wd_ref):
        self.h = h
        self.refs = (gx_ref, wq_ref, k_ref, v_ref, wo_ref, gm_ref, wu_ref, wd_ref)
        self.ff_chunk = ff_chunk
        self.num_chunks = wu_ref.shape[1] // ff_chunk

    def attention(self):
        gx_ref, wq_ref, k_ref, v_ref, wo_ref, gm_ref, _, _ = self.refs
        xn = _rms_norm(self.h, gx_ref[...]).astype(BF16)
        q = (_dot(xn, wq_ref[...]) * (XA_HEAD_DIM ** -0.5)).astype(BF16)
        outs = []
        for hd in range(XA_HEADS):
            cols = slice(hd * XA_HEAD_DIM, (hd + 1) * XA_HEAD_DIM)
            s = _dot_nt(q[:, cols], k_ref[0, :, cols])
            p = jnp.exp(s - jnp.max(s, axis=-1, keepdims=True))
            denom = jnp.sum(p, axis=-1, keepdims=True)
            outs.append((_dot(p.astype(BF16), v_ref[0, :, cols]) / denom).astype(BF16))
        self.acc = self.h + _dot(jnp.concatenate(outs, axis=-1), wo_ref[...])
        self.xm = _rms_norm(self.acc, gm_ref[...]).astype(BF16)
        self.hid = self._up(0)
        return self.acc[0:1, 0:128]

    def _up(self, i):
        wu_ref = self.refs[6]
        hid = _dot(self.xm, wu_ref[:, i * self.ff_chunk:(i + 1) * self.ff_chunk])
        return jnp.square(jnp.maximum(hid, 0.0)).astype(BF16)

    def mlp_chunk(self, i):
        wd_ref = self.refs[7]
        hid = self.hid
        if i + 1 < self.num_chunks:
            self.hid = self._up(i + 1)
        self.acc = self.acc + _dot(hid, wd_ref[i * self.ff_chunk:(i + 1) * self.ff_chunk, :])
        return self.acc[0:1, 0:128]


def _run_lagged(s, tiles, lag_ref, lag_refs, gf_ref, o_ref, mixer, *, final_norm, ff_chunk):
    @pl.when(s == 0)
    def _():
        lag_ref[1] = jnp.zeros(lag_ref.shape[1:], F32)

    stream = _AttnMlp(lag_ref[(s + 1) % 2], ff_chunk, *lag_refs)
    stages = [stream.attention] + [functools.partial(stream.mlp_chunk, i) for i in range(stream.num_chunks)]
    lag_ref[s % 2] = mixer(stages)
    for stage in stages:
        stage()
    out = stream.acc
    o_ref[...] = _rms_norm(out, gf_ref[...]) if final_norm else out


def _take(stages, n):
    token = None
    for _ in range(min(n, len(stages))):
        token = stages.pop(0)()
    if token is None:
        return jnp.zeros((1, 128), F32)
    return _zero_row(token)


def _zero_row(token):
    bits = lax.bitcast_convert_type(token, jnp.uint32)
    bits = lax.shift_right_logical(lax.shift_right_logical(bits, jnp.uint32(16)), jnp.uint32(16))
    return lax.bitcast_convert_type(bits, F32)


def _conformer_mixer(x, first, refs, wins, c_ref, stages):
    g_ref, w1_ref, b1_ref, wdw_ref, bdw_ref, lg_ref, lb_ref, w2_ref, b2_ref = refs
    ts = x.shape[0]

    @pl.when(first)
    def _():
        for win_ref in wins:
            win_ref[0:CV_HALO, :] = jnp.zeros((CV_HALO, CV_WIN_COLS), F32)

    xn = _rms_norm(x, g_ref[...]).astype(BF16)
    for i, win_ref in enumerate(wins):
        val = slice(i * CV_WIN_COLS, (i + 1) * CV_WIN_COLS)
        gat = slice(D_MODEL + i * CV_WIN_COLS, D_MODEL + (i + 1) * CV_WIN_COLS)
        gate = _dot(xn, w1_ref[:, gat]) + b1_ref[:, gat]
        win_ref[CV_HALO:, :] = (_dot(xn, w1_ref[:, val]) + b1_ref[:, val]) * _sigmoid(gate)
    base = CV_HALO - (CV_WIDTH - 1)
    done = []
    for cs in range(D_MODEL // 128):
        cols = slice(cs * 128, (cs + 1) * 128)
        taps = wdw_ref[:, cols] if cs < CV_TIE_BACK else wdw_ref[:, cols] + done[cs - CV_TIE_BACK]
        win_ref = wins[cs * 128 // CV_WIN_COLS]
        wcols = slice(cs * 128 % CV_WIN_COLS, cs * 128 % CV_WIN_COLS + 128)
        for r0 in range(0, ts, CV_ROWS):
            partial = {}
            for j in range(CV_WIDTH):
                start = r0 + base + j
                term = taps[j:j + 1] * win_ref[start:start + CV_ROWS, wcols]
                phase = start % 8
                partial[phase] = partial[phase] + term if phase in partial else term
            acc = partial[0] + bdw_ref[0:1, cols]
            for phase in range(1, 8):
                acc = acc + partial[phase]
            c_ref[r0:r0 + CV_ROWS, cols] = acc
        done.append(_take(stages, 2))

    for win_ref in wins:
        win_ref[0:CV_HALO, :] = win_ref[ts:ts + CV_HALO, :]

    c = c_ref[...]
    mu = jnp.mean(c, axis=-1, keepdims=True)
    xc = c - mu
    gain = lg_ref[...] + jnp.tile(done[-CV_TIE_BACK], (1, D_MODEL // 128))
    y = xc * lax.rsqrt(jnp.mean(xc * xc, axis=-1, keepdims=True) + LN_EPS) * gain + lb_ref[...]
    y = (y * _sigmoid(y)).astype(BF16)
    return x + _dot(y, w2_ref[...]) + b2_ref[...]


def _conformer_layer_kernel(*refs, tiles, final_norm):
    x_ref = refs[0]
    mixer_refs = refs[1:10]
    lag_refs = refs[10:18]
    gf_ref, o_ref = refs[18:20]
    num_wins = D_MODEL // CV_WIN_COLS
    wins = refs[20:20 + num_wins]
    c_ref, lag_ref = refs[20 + num_wins:]
    s = pl.program_id(0)
    mixer = functools.partial(_conformer_mixer, x_ref[...], s % tiles == 0, mixer_refs, wins, c_ref)
    _run_lagged(s, tiles, lag_ref, lag_refs, gf_ref, o_ref, mixer, final_norm=final_norm,
                ff_chunk=CV_FF_CHUNK)


def _delta_mixer(x, first, refs, wins, z_ref, state_ref, gated_ref, stages):
    g_ref, wi_ref, wc_ref, gp_ref, on_ref, wo_ref, bd_ref = refs
    nt = x.shape[0]
    ch = DN_CHUNK
    nchunks = nt // ch
    heads = range(DN_HEADS)

    @pl.when(first)
    def _():
        for win_ref in wins:
            win_ref[0:8, :] = jnp.zeros((8, DN_WIN_COLS), F32)
        state_ref[...] = jnp.zeros(state_ref.shape, F32)

    heads_per_win = DN_WIN_COLS // DN_HEAD_DIM

    def conv_silu(part, h):
        win_ref = wins[3 * (h // heads_per_win) + part]
        cols = slice((h % heads_per_win) * DN_HEAD_DIM, (h % heads_per_win + 1) * DN_HEAD_DIM)
        wcols = slice(part * DN_INNER + h * DN_HEAD_DIM, part * DN_INNER + (h + 1) * DN_HEAD_DIM)
        acc = wc_ref[0:1, wcols] * win_ref[5:5 + nt, cols]
        for j in range(1, DN_CONV):
            acc = acc + wc_ref[j:j + 1, wcols] * win_ref[5 + j:5 + j + nt, cols]
        return acc * _sigmoid(acc)

    qkv = {}

    def conv_norm(pair):
        for h in range(pair * heads_per_win, (pair + 1) * heads_per_win):
            q = conv_silu(0, h)
            k = conv_silu(1, h)
            v = conv_silu(2, h)
            q = q * (lax.rsqrt(jnp.sum(q * q, axis=-1, keepdims=True) + L2_EPS) * (DN_HEAD_DIM ** -0.5))
            k = k * lax.rsqrt(jnp.sum(k * k, axis=-1, keepdims=True) + L2_EPS)
            qkv[h] = (q, k, v)

    xn = _rms_norm(x, g_ref[...]).astype(BF16)
    ba = _dot(xn, wi_ref[:, 4 * DN_INNER:])
    num_pairs = DN_INNER // DN_WIN_COLS
    for pair in range(num_pairs):
        for part in range(3):
            col0 = part * DN_INNER + pair * DN_WIN_COLS
            wins[3 * pair + part][8:8 + nt, :] = _dot(xn, wi_ref[:, col0:col0 + DN_WIN_COLS])
        if pair >= 1:
            conv_norm(pair - 1)
    z_ref[...] = _dot(xn, wi_ref[:, 3 * DN_INNER:4 * DN_INNER])
    conv_norm(num_pairs - 1)

    beta_all = _sigmoid(ba)
    ap = ba + gp_ref[1:2, :]
    softplus = jnp.maximum(ap, 0.0) + jnp.log(1.0 + jnp.exp(-jnp.abs(ap)))
    g_all = -jnp.exp(gp_ref[0:1, :]) * softplus
    pos = lax.broadcasted_iota(jnp.int32, (nt, 128), 0) % ch
    gc_all = g_all
    rs_all = g_all
    shift = 1
    while shift < ch:
        gc_all = gc_all + jnp.where(pos >= shift, pltpu.roll(gc_all, shift, axis=0), 0.0)
        rs_all = rs_all + jnp.where(pos < ch - shift, pltpu.roll(rs_all, nt - shift, axis=0), 0.0)
        shift *= 2
    rem_all = rs_all - g_all
    gc_rows = jnp.transpose(gc_all)

    side_row = lax.broadcasted_iota(jnp.int32, (ch, nt), 0)
    side_col = lax.broadcasted_iota(jnp.int32, (ch, nt), 1)
    col_chunk = side_col // ch
    causal_side = side_row >= side_col % ch
    strict_side = side_row > side_col % ch
    eye_side = (side_row == side_col % ch).astype(F32)

    def to_blockdiag(x_side):
        return jnp.concatenate([x_side.astype(BF16)] * nchunks, axis=0) * bd_ref[...]

    def fold(m):
        out = m[(nchunks - 1) * ch:]
        for c in reversed(range(nchunks - 1)):
            out = jnp.where(col_chunk == c, m[c * ch:(c + 1) * ch], out)
        return out

    def side_columns(col):
        out = jnp.broadcast_to(col[(nchunks - 1) * ch:], (ch, nt))
        for c in reversed(range(nchunks - 1)):
            out = jnp.where(col_chunk == c, col[c * ch:(c + 1) * ch], out)
        return out

    gcc, lower, attn, rhs, qd, kd = [], [], [], [], [], []
    for h in heads:
        beta = beta_all[:, h:h + 1]
        gcc.append(gc_all[:, DN_HEADS + h:DN_HEADS + h + 1])
        rem = rem_all[:, DN_HEADS + h:DN_HEADS + h + 1]
        gcr = gc_rows[DN_HEADS + h:DN_HEADS + h + 1, :]

        q, k, v = qkv[h]
        decay = jnp.exp(jnp.where(causal_side, side_columns(gcc[h]) - gcr, -jnp.inf))
        egc = jnp.exp(gcc[h])
        kb = k * beta
        both = _dot(jnp.concatenate([kb, q], axis=0).astype(BF16), jnp.transpose(k).astype(BF16))
        lower.append(jnp.where(strict_side, fold(both[:nt]) * decay, 0.0))
        attn.append(to_blockdiag(fold(both[nt:]) * decay))
        rhs.append(jnp.concatenate([v * beta, kb * egc], axis=1))
        qd.append((q * egc).astype(BF16))
        kd.append((k * jnp.exp(rem)).astype(BF16))
        _take(stages, 1)

    inv = [eye_side - lower[h] for h in heads]
    power = [_dot(lower[h].astype(BF16), to_blockdiag(lower[h])) for h in heads]
    order = 2
    while order < ch:
        last = order * 2 >= ch
        for h in heads:
            wts = to_blockdiag(power[h])
            if last:
                inv[h] = inv[h] + _dot(inv[h].astype(BF16), wts)
            else:
                res = _dot(jnp.concatenate([inv[h], power[h]], axis=0).astype(BF16), wts)
                inv[h] = inv[h] + res[:ch]
                power[h] = res[ch:]
        order *= 2
    _take(stages, 1)

    u, wbf = [], []
    for h in heads:
        sol = rhs[h] + _dot(to_blockdiag(inv[h] - eye_side), rhs[h].astype(BF16))
        u.append(sol[:, :DN_HEAD_DIM])
        wbf.append(sol[:, DN_HEAD_DIM:].astype(BF16))
    _take(stages, 1)

    state = [state_ref[h] for h in heads]
    v_new = [[] for _ in heads]
    o_inter = [[] for _ in heads]
    for c in range(nchunks):
        rows = slice(c * ch, (c + 1) * ch)
        last_row = (c + 1) * ch - 1
        for h in heads:
            both = _dot(jnp.concatenate([wbf[h][rows], qd[h][rows]], axis=0), state[h].astype(BF16))
            vn = u[h][rows] - both[:ch]
            o_inter[h].append(both[ch:])
            v_new[h].append(vn)
            state[h] = (state[h] * jnp.exp(gcc[h][last_row:last_row + 1, :])
                        + _dot_tn(kd[h][rows], vn.astype(BF16)))

    for h in heads:
        state_ref[h] = state[h]
        vn = jnp.concatenate(v_new[h], axis=0).astype(BF16)
        o = jnp.concatenate(o_inter[h], axis=0) + _dot(attn[h], vn)
        o = o * lax.rsqrt(jnp.mean(o * o, axis=-1, keepdims=True) + RMS_EPS) * on_ref[...]
        zc = z_ref[:, h * DN_HEAD_DIM:(h + 1) * DN_HEAD_DIM]
        gated_ref[:, h * DN_HEAD_DIM:(h + 1) * DN_HEAD_DIM] = (o * (zc * _sigmoid(zc))).astype(BF16)

    for win_ref in wins:
        win_ref[0:8, :] = win_ref[nt:nt + 8, :]
    return x + _dot(gated_ref[...], wo_ref[...])


def _deltanet_layer_kernel(*refs, tiles, final_norm):
    x_ref = refs[0]
    mixer_refs = refs[1:8]
    lag_refs = refs[8:16]
    gf_ref, o_ref = refs[16:18]
    num_wins = 3 * DN_INNER // DN_WIN_COLS
    wins = refs[18:18 + num_wins]
    z_ref, state_ref, gated_ref, lag_ref = refs[18 + num_wins:]
    s = pl.program_id(0)
    mixer = functools.partial(_delta_mixer, x_ref[...], s % tiles == 0, mixer_refs,
                              wins, z_ref, state_ref, gated_ref)
    _run_lagged(s, tiles, lag_ref, lag_refs, gf_ref, o_ref, mixer, final_norm=final_norm,
                ff_chunk=DN_FF_CHUNK)


def _layer_call(body, h2d, mixer_args, lag_args, kv, gf, scratch, *, tile, batch, seq, final_norm, name,
                kv_row0):
    tiles = seq // tile
    steps = batch * tiles
    d = h2d.shape[1]
    mlen = kv.shape[1]
    gx, wq, wo, gm, wu, wd = lag_args

    def const(a):
        return pl.BlockSpec(a.shape, lambda s: (0,) * a.ndim)

    row = lambda v: v.reshape(1, -1)
    lag_operands = [row(gx), wq, kv, kv, wo, row(gm), wu, wd]
    lag_batch = lambda s: kv_row0 + jnp.maximum(s - 1, 0) // tiles
    lag_specs = [
        const(lag_operands[0]), const(wq),
        pl.BlockSpec((1, mlen, d), lambda s: (lag_batch(s), 0, 0)),
        pl.BlockSpec((1, mlen, d), lambda s: (lag_batch(s), 0, 1)),
        const(wo), const(lag_operands[5]), const(wu), const(wd),
    ]
    return pl.pallas_call(
        functools.partial(body, tiles=tiles, final_norm=final_norm),
        grid=(steps + 1,),
        in_specs=[pl.BlockSpec((tile, d), lambda s: (jnp.minimum(s, steps - 1), 0))]
                 + [const(a) for a in mixer_args] + lag_specs + [const(row(gf))],
        out_specs=pl.BlockSpec((tile, d), lambda s: (jnp.maximum(s - 1, 0), 0)),
        out_shape=jax.ShapeDtypeStruct(h2d.shape, F32),
        scratch_shapes=scratch + [pltpu.VMEM((2, tile, d), F32)],
        compiler_params=_params("arbitrary"),
        name=name,
    )(h2d, *mixer_args, *lag_operands, row(gf))


def kernel(x, mem, dn_norm, dn_w_in, dn_w_conv, dn_a_log, dn_dt_bias, dn_out_norm, dn_w_out, cv_norm, cv_w_pw1, cv_b_pw1, cv_w_dw, cv_b_dw, cv_ln_g, cv_ln_b, cv_w_pw2, cv_b_pw2, xa_norm, xa_mem_norm, xa_w_q, xa_w_kv, xa_w_o, mlp_norm, mlp_w_up, mlp_w_down, final_norm):
    batch, seq, d = x.shape
    depth = xa_norm.shape[0]
    mlen = mem.shape[1]
    h = x.reshape(batch * seq, d)
    mem2d = mem.reshape(batch * mlen, d)
    row = lambda v: v.reshape(1, -1)
    chunk_of = jnp.arange(DN_TILE) // DN_CHUNK
    blockdiag_mask = (chunk_of[:, None] == chunk_of[None, :]).astype(BF16)
    kv_all = rms_matmul_layers(mem2d, xa_mem_norm, xa_w_kv.astype(BF16), tm=KV_TILE, tn=KV_TILE,
                               out_dtype=BF16, name="kv_all_layers")
    for layer in range(depth):
        j = layer // 2
        kv = kv_all.reshape(depth * batch, mlen, 2 * d)
        lag_args = (xa_norm[layer], xa_w_q[layer].astype(BF16), xa_w_o[layer].astype(BF16),
                    mlp_norm[layer], mlp_w_up[layer].astype(BF16), mlp_w_down[layer].astype(BF16))
        common = dict(batch=batch, seq=seq, final_norm=layer == depth - 1, kv_row0=layer * batch)
        if layer % 2 == 0:
            w_in = jnp.pad(dn_w_in[j], ((0, 0), (0, DN_PROJ_COLS - dn_w_in.shape[2]))).astype(BF16)
            gate_params = jnp.zeros((8, 128), F32)
            gate_params = gate_params.at[0, DN_HEADS:2 * DN_HEADS].set(dn_a_log[j])
            gate_params = gate_params.at[1, DN_HEADS:2 * DN_HEADS].set(dn_dt_bias[j])
            mixer_args = (row(dn_norm[j]), w_in, dn_w_conv[j], gate_params, row(dn_out_norm[j]),
                          dn_w_out[j].astype(BF16), blockdiag_mask)
            scratch = [pltpu.VMEM((DN_TILE + 8, DN_WIN_COLS), F32)] * (3 * DN_INNER // DN_WIN_COLS) + [
                pltpu.VMEM((DN_TILE, DN_INNER), F32),
                pltpu.VMEM((DN_HEADS, DN_HEAD_DIM, DN_HEAD_DIM), F32),
                pltpu.VMEM((DN_TILE, DN_INNER), BF16),
            ]
            h = _layer_call(_deltanet_layer_kernel, h, mixer_args, lag_args, kv, final_norm, scratch,
                            tile=DN_TILE, name=f"deltanet_layer{layer}", **common)
        else:
            mixer_args = (row(cv_norm[j]), cv_w_pw1[j].astype(BF16), row(cv_b_pw1[j]),
                          cv_w_dw[j], row(cv_b_dw[j]), row(cv_ln_g[j]), row(cv_ln_b[j]),
                          cv_w_pw2[j].astype(BF16), row(cv_b_pw2[j]))
            scratch = ([pltpu.VMEM((CV_TILE + CV_HALO, CV_WIN_COLS), F32)] * (d // CV_WIN_COLS)
                       + [pltpu.VMEM((CV_TILE, d), F32)])
            h = _layer_call(_conformer_layer_kernel, h, mixer_args, lag_args, kv, final_norm, scratch,
                            tile=CV_TILE, name=f"conformer_layer{layer}", **common)
    return h.reshape(batch, seq, d)
```
